```python
import math
import jax
import jax.numpy as jnp
from jax import lax

D_MODEL = 2048
BATCH = 2
SEQ = 4096
DEPTH = 2
DEC_BATCH = 128
DEC_SEQ = 4
PAST_LEN = 16384
PAGE_SIZE = 128

N_A_LAYERS = DEPTH // 2
N_B_LAYERS = DEPTH - N_A_LAYERS
H_A = 16
Q_LORA = 512
KV_LORA = 512
QK_NOPE = 128
QK_ROPE = 64
V_DIM = 128
LATENT_W = KV_LORA + QK_ROPE
MLA_SCALE = 1.0 / math.sqrt(QK_NOPE + QK_ROPE)
ROPE_THETA = 10000.0
Q_BLOCK = 128
H_B = 32
N_KV_B = 4
GROUP_B = H_B // N_KV_B
HD_B = 64
SWA_SCALE = 1.0 / math.sqrt(HD_B)
WINDOW = 128
N_BUCKETS = 32
MAX_DISTANCE = 128
D_FF = -(-8 * D_MODEL // (3 * 256)) * 256
EPS = 1e-6
NEG_INF = -1e30

kernel_name = 'yoco_mla_swa_sink_decoder_step'


def rmsnorm(x, g):
    xf = x.astype(jnp.float32)
    y = xf * lax.rsqrt(jnp.mean(xf * xf, axis=-1, keepdims=True) + EPS)
    return (y * g.astype(jnp.float32)).astype(x.dtype)


def swiglu(x, w_gate, w_up, w_down):
    return (jax.nn.silu(x @ w_gate) * (x @ w_up)) @ w_down


def rope_tables(pos):
    inv_freq = ROPE_THETA ** (-jnp.arange(0, QK_ROPE, 2, dtype=jnp.float32) / QK_ROPE)
    ang = pos.astype(jnp.float32)[:, None] * inv_freq[None, :]
    return jnp.cos(ang), jnp.sin(ang)


def apply_rope(x, cos, sin):
    xf = x.astype(jnp.float32)
    x1, x2 = jnp.split(xf, 2, axis=-1)
    return jnp.concatenate([x1 * cos - x2 * sin, x2 * cos + x1 * sin], axis=-1).astype(x.dtype)


def rel_bucket(dist):
    n = jnp.maximum(dist, 0)
    max_exact = N_BUCKETS // 2
    nf = jnp.maximum(n, 1).astype(jnp.float32)
    large = max_exact + (jnp.log(nf / max_exact) / math.log(MAX_DISTANCE / max_exact)
                         * (N_BUCKETS - max_exact)).astype(jnp.int32)
    large = jnp.minimum(large, N_BUCKETS - 1)
    return jnp.where(n < max_exact, n, large)


def mla_project(xn, pos, wq_a, q_norm, wq_b, wkv_a, kv_norm, wkv_b):
    cq = rmsnorm(xn @ wq_a, q_norm)
    q = jnp.einsum('btr,rhe->bthe', cq, wq_b)
    q_nope, q_pe = q[..., :QK_NOPE], q[..., QK_NOPE:]
    cos, sin = rope_tables(pos)
    q_pe = apply_rope(q_pe, cos[:, None, :], sin[:, None, :])
    kv = xn @ wkv_a
    c_kv = rmsnorm(kv[..., :KV_LORA], kv_norm)
    k_pe = apply_rope(kv[..., KV_LORA:], cos, sin)
    q_lat = jnp.einsum('bthn,chn->bthc', q_nope, wkv_b[..., :QK_NOPE])
    q_full = jnp.concatenate([q_lat, q_pe.astype(q_lat.dtype)], axis=-1)
    rows = jnp.concatenate([c_kv, k_pe.astype(c_kv.dtype)], axis=-1)
    return q_full, rows


def mla_attend(q_full, rows, q_pos, k_pos):
    s = jnp.einsum('...qhc,...kc->...hqk', q_full, rows).astype(jnp.float32) * MLA_SCALE
    s = jnp.where(k_pos[None, :] <= q_pos[:, None], s, NEG_INF)
    p = jax.nn.softmax(s, axis=-1)
    return jnp.einsum('...hqk,...kc->...qhc', p.astype(rows.dtype), rows[..., :KV_LORA])


def mla_prompt_attend(q_full, rows):
    b, s = q_full.shape[0], q_full.shape[1]
    nblk = s // Q_BLOCK
    k_pos = jnp.arange(s, dtype=jnp.int32)
    q_blocks = q_full.reshape(b, nblk, Q_BLOCK, H_A, LATENT_W).swapaxes(0, 1)
    pos_blocks = k_pos.reshape(nblk, Q_BLOCK)

    def block(args):
        qb, qpos = args
        return mla_attend(qb, rows, qpos, k_pos)

    out = lax.map(block, (q_blocks, pos_blocks))
    return out.swapaxes(0, 1).reshape(b, s, H_A, KV_LORA)


def mla_sample_attend(q_full, new_rows, cache_mla, layer, page_table):
    t = q_full.shape[1]
    past_len = page_table.shape[1] * PAGE_SIZE
    q_pos = past_len + jnp.arange(t, dtype=jnp.int32)
    k_pos = jnp.arange(past_len + t, dtype=jnp.int32)

    def one_seq(args):
        qs, new_s, pages = args
        past = cache_mla[layer, pages].reshape(past_len, LATENT_W)
        rows = jnp.concatenate([past, new_s.astype(past.dtype)], axis=0)
        return mla_attend(qs, rows, q_pos, k_pos)

    return lax.map(one_seq, (q_full, new_rows, page_table))


def mla_output(out_lat, wkv_b, wo):
    v = jnp.einsum('bthc,chv->bthv', out_lat, wkv_b[..., QK_NOPE:])
    return jnp.einsum('bthv,hvd->btd', v, wo)


def shared_kv(h, g, w_k, b_k, w_v, b_v):
    hn = rmsnorm(h, g)
    k = jnp.einsum('btd,dne->btne', hn, w_k) + b_k
    v = jnp.einsum('btd,dne->btne', hn, w_v) + b_v
    return jnp.stack([k, v], axis=2)


def swa_attend(q, k, v, q_pos, k_pos, sinks, rel_bias):
    s = jnp.einsum('...qngd,...knd->...ngqk', q, k).astype(jnp.float32) * SWA_SCALE
    dist = q_pos[..., :, None] - k_pos[..., None, :]
    bias = jnp.take(rel_bias, rel_bucket(dist), axis=0)
    bias = jnp.moveaxis(bias, -1, -3)
    bias = bias.reshape(bias.shape[:-3] + (N_KV_B, GROUP_B) + bias.shape[-2:])
    valid = (dist >= 0) & (dist < WINDOW) & (k_pos[..., None, :] >= 0)
    s = jnp.where(valid[..., None, None, :, :], s + bias.astype(jnp.float32), NEG_INF)
    sink = sinks.astype(jnp.float32).reshape(N_KV_B, GROUP_B, 1, 1)
    m = jnp.maximum(s.max(axis=-1, keepdims=True), sink)
    p = jnp.exp(s - m)
    p = p / (p.sum(axis=-1, keepdims=True) + jnp.exp(sink - m))
    return jnp.einsum('...ngqk,...knd->...qngd', p.astype(v.dtype), v)


def swa_prompt(xn, kv, wq, bq, sinks, rel_bias):
    b, s, _ = xn.shape
    nblk = s // WINDOW
    q = (jnp.einsum('btd,dhe->bthe', xn, wq) + bq).reshape(b, nblk, WINDOW, N_KV_B, GROUP_B, HD_B)
    kvp = jnp.pad(kv, ((0, 0), (WINDOW, 0), (0, 0), (0, 0), (0, 0)))
    kvp = kvp.reshape(b, nblk + 1, WINDOW, 2, N_KV_B, HD_B)
    band = jnp.concatenate([kvp[:, :-1], kvp[:, 1:]], axis=2)
    q_pos = jnp.arange(s, dtype=jnp.int32).reshape(nblk, WINDOW)
    k_pos = (jnp.arange(nblk, dtype=jnp.int32) * WINDOW - WINDOW)[:, None] + jnp.arange(2 * WINDOW, dtype=jnp.int32)[None, :]
    o = swa_attend(q, band[..., 0, :, :], band[..., 1, :, :], q_pos, k_pos, sinks, rel_bias)
    return o.reshape(b, s, H_B, HD_B)


def swa_sample(xn, kv_full, q_pos, k_pos, wq, bq, sinks, rel_bias):
    db, t, _ = xn.shape
    q = (jnp.einsum('btd,dhe->bthe', xn, wq) + bq).reshape(db, t, N_KV_B, GROUP_B, HD_B)
    o = swa_attend(q, kv_full[:, :, 0], kv_full[:, :, 1], q_pos, k_pos, sinks, rel_bias)
    return o.reshape(db, t, H_B, HD_B)


def swa_output(o, wo, bo):
    return jnp.einsum('bthe,hed->btd', o, wo) + bo


def setup_inputs(seed: int = 0) -> dict:
    key = jax.random.key(seed)
    ks = jax.random.split(key, 32)

    def nrm(k, shape, scale):
        return jax.random.normal(k, shape, jnp.float32) * scale

    n_pages = PAST_LEN // PAGE_SIZE
    n_used = DEC_BATCH * n_pages
    n_pool = n_used + n_used // 4
    n_win = min(WINDOW, PAST_LEN)
    page_table = jax.random.permutation(ks[4], n_pool)[:n_used].reshape(DEC_BATCH, n_pages).astype(jnp.int32)
    return {
        'x_prompt': nrm(ks[0], (BATCH, SEQ, D_MODEL), 1.0),
        'x_sample': nrm(ks[1], (DEC_BATCH, DEC_SEQ, D_MODEL), 1.0),
        'cache_mla': nrm(ks[2], (N_A_LAYERS, n_pool, PAGE_SIZE, LATENT_W), 1.0),
        'state_kv_win': nrm(ks[3], (DEC_BATCH, n_win, 2, N_KV_B, HD_B), 1.0),
        'page_table': page_table,
        'attn_norm': 1.0 + nrm(ks[5], (DEPTH, D_MODEL), 0.02),
        'ffn_norm': 1.0 + nrm(ks[6], (DEPTH, D_MODEL), 0.02),
        'final_norm': 1.0 + nrm(ks[7], (D_MODEL,), 0.02),
        'mla_wq_a': nrm(ks[8], (N_A_LAYERS, D_MODEL, Q_LORA), D_MODEL ** -0.5),
        'mla_q_norm': 1.0 + nrm(ks[9], (N_A_LAYERS, Q_LORA), 0.02),
        'mla_wq_b': nrm(ks[10], (N_A_LAYERS, Q_LORA, H_A, QK_NOPE + QK_ROPE), Q_LORA ** -0.5),
        'mla_wkv_a': nrm(ks[11], (N_A_LAYERS, D_MODEL, LATENT_W), D_MODEL ** -0.5),
        'mla_kv_norm': 1.0 + nrm(ks[12], (N_A_LAYERS, KV_LORA), 0.02),
        'mla_wkv_b': nrm(ks[13], (N_A_LAYERS, KV_LORA, H_A, QK_NOPE + V_DIM), KV_LORA ** -0.5),
        'mla_wo': nrm(ks[14], (N_A_LAYERS, H_A, V_DIM, D_MODEL), (H_A * V_DIM) ** -0.5),
        'kv_norm_shared': 1.0 + nrm(ks[15], (D_MODEL,), 0.02),
        'w_k_shared': nrm(ks[16], (D_MODEL, N_KV_B, HD_B), D_MODEL ** -0.5),
        'b_k_shared': nrm(ks[17], (N_KV_B, HD_B), 0.02),
        'w_v_shared': nrm(ks[18], (D_MODEL, N_KV_B, HD_B), D_MODEL ** -0.5),
        'b_v_shared': nrm(ks[19], (N_KV_B, HD_B), 0.02),
        'swa_wq': nrm(ks[20], (N_B_LAYERS, D_MODEL, H_B, HD_B), D_MODEL ** -0.5),
        'swa_bq': nrm(ks[21], (N_B_LAYERS, H_B, HD_B), 0.02),
        'swa_sinks': nrm(ks[22], (N_B_LAYERS, H_B), 0.5),
        'swa_wo': nrm(ks[23], (N_B_LAYERS, H_B, HD_B, D_MODEL), (H_B * HD_B) ** -0.5),
        'swa_bo': nrm(ks[24], (N_B_LAYERS, D_MODEL), 0.02),
        'rel_bias': nrm(ks[25], (N_BUCKETS, H_B), 0.5),
        'ffn_w_gate': nrm(ks[26], (DEPTH, D_MODEL, D_FF), D_MODEL ** -0.5),
        'ffn_w_up': nrm(ks[27], (DEPTH, D_MODEL, D_FF), D_MODEL ** -0.5),
        'ffn_w_down': nrm(ks[28], (DEPTH, D_FF, D_MODEL), D_FF ** -0.5),
    }


def reference(x_prompt, x_sample, cache_mla, state_kv_win, page_table,
              attn_norm, ffn_norm, final_norm,
              mla_wq_a, mla_q_norm, mla_wq_b, mla_wkv_a, mla_kv_norm, mla_wkv_b, mla_wo,
              kv_norm_shared, w_k_shared, b_k_shared, w_v_shared, b_v_shared,
              swa_wq, swa_bq, swa_sinks, swa_wo, swa_bo, rel_bias,
              ffn_w_gate, ffn_w_up, ffn_w_down):
    seq = x_prompt.shape[1]
    dec_seq = x_sample.shape[1]
    past_len = page_table.shape[1] * PAGE_SIZE
    n_win = state_kv_win.shape[1]
    pos_p = jnp.arange(seq, dtype=jnp.int32)
    pos_s = past_len + jnp.arange(dec_seq, dtype=jnp.int32)
    k_pos_s = past_len - n_win + jnp.arange(n_win + dec_seq, dtype=jnp.int32)

    h_p, h_s = x_prompt, x_sample
    rows_p_all, rows_s_all = [], []
    kv_p, kv_s_full = None, None
    for layer in range(DEPTH):
        xn_p = rmsnorm(h_p, attn_norm[layer])
        xn_s = rmsnorm(h_s, attn_norm[layer])
        if layer < N_A_LAYERS:
            a = layer
            w = (mla_wq_a[a], mla_q_norm[a], mla_wq_b[a], mla_wkv_a[a], mla_kv_norm[a], mla_wkv_b[a])
            q_p, rows_p = mla_project(xn_p, pos_p, *w)
            q_s, rows_s = mla_project(xn_s, pos_s, *w)
            o_p = mla_prompt_attend(q_p, rows_p)
            o_s = mla_sample_attend(q_s, rows_s, cache_mla, a, page_table)
            h_p = h_p + mla_output(o_p, mla_wkv_b[a], mla_wo[a])
            h_s = h_s + mla_output(o_s, mla_wkv_b[a], mla_wo[a])
            rows_p_all.append(rows_p)
            rows_s_all.append(rows_s)
        else:
            b = layer - N_A_LAYERS
            if b == 0:
                kv_p = shared_kv(h_p, kv_norm_shared, w_k_shared, b_k_shared, w_v_shared, b_v_shared)
                kv_s = shared_kv(h_s, kv_norm_shared, w_k_shared, b_k_shared, w_v_shared, b_v_shared)
                kv_s_full = jnp.concatenate([state_kv_win, kv_s.astype(state_kv_win.dtype)], axis=1)
            o_p = swa_prompt(xn_p, kv_p, swa_wq[b], swa_bq[b], swa_sinks[b], rel_bias)
            o_s = swa_sample(xn_s, kv_s_full, pos_s, k_pos_s, swa_wq[b], swa_bq[b], swa_sinks[b], rel_bias)
            h_p = h_p + swa_output(o_p, swa_wo[b], swa_bo[b])
            h_s = h_s + swa_output(o_s, swa_wo[b], swa_bo[b])
        h_p = h_p + swiglu(rmsnorm(h_p, ffn_norm[layer]), ffn_w_gate[layer], ffn_w_up[layer], ffn_w_down[layer])
        h_s = h_s + swiglu(rmsnorm(h_s, ffn_norm[layer]), ffn_w_gate[layer], ffn_w_up[layer], ffn_w_down[layer])

    y_prompt = rmsnorm(h_p, final_norm)
    y_sample = rmsnorm(h_s, final_norm)
    mla_rows_prompt = jnp.stack(rows_p_all, axis=0)
    mla_rows_sample = jnp.stack(rows_s_all, axis=0)
    kv_win_prompt = kv_p[:, seq - min(WINDOW, seq):]
    kv_win_sample = kv_s_full[:, dec_seq:]
    return (y_prompt, y_sample, mla_rows_prompt, mla_rows_sample, kv_win_prompt, kv_win_sample)
```

```python
import functools
import math

import jax
import jax.numpy as jnp
from jax import lax
from jax.experimental import pallas as pl
from jax.experimental.pallas import tpu as pltpu

F32 = jnp.float32
BF16 = jnp.bfloat16

EPS = 1e-6
NEG_INF = -1e30
ROPE_THETA = 10000.0
PAGE_SIZE = 128
WINDOW = 128
N_BUCKETS = 32
MAX_DISTANCE = 128

VMEM_LIMIT_BYTES = 52 * 1024 * 1024
ROW_TILE = 512
FFN_TILE = 512
MLA_Q_TILE = 128
MLA_K_TILE = 512
PAGES_PER_STEP = 8
NEW_ROWS_PAD = 16
SWA_SEQS_PER_STEP = 8


def _params(*semantics):
    return pltpu.CompilerParams(dimension_semantics=semantics, vmem_limit_bytes=VMEM_LIMIT_BYTES)


def _rms(x, g):
    return x * lax.rsqrt(jnp.mean(x * x, axis=-1, keepdims=True) + EPS) * g


def _dot(a, b):
    return jnp.dot(a, b, preferred_element_type=F32)


def _dot_nt(a, b):
    return lax.dot_general(a, b, (((1,), (1,)), ((), ())), preferred_element_type=F32)


def _resident(shape):
    return pl.BlockSpec(shape, lambda *_: (0,) * len(shape))


def _mla_in_kernel(h_ref, g_ref, w_ref, qn_ref, kvn_ref, cos_ref, sin_ref,
                   cq_ref, rows_ref, rowsb_ref, *, q_lora, kv_lora, rope):
    xn = _rms(h_ref[...], g_ref[...]).astype(BF16)
    y = _dot(xn, w_ref[...])
    cq_ref[...] = _rms(y[:, :q_lora], qn_ref[...]).astype(BF16)
    c0 = q_lora
    ckv = _rms(y[:, c0:c0 + kv_lora], kvn_ref[...])
    p0 = c0 + kv_lora
    kpe = y[:, p0:p0 + rope] * cos_ref[...] + y[:, p0 + rope:p0 + 2 * rope] * sin_ref[...]
    rows_ref[:, :kv_lora] = ckv
    rows_ref[:, kv_lora:] = kpe
    rowsb_ref[:, :kv_lora] = ckv.astype(BF16)
    rowsb_ref[:, kv_lora:] = kpe.astype(BF16)


def _mla_in(h, g, w, qn, kvn, cos2, sin2, *, q_lora, kv_lora, rope):
    m, d = h.shape
    tm = min(ROW_TILE, m)
    lat = kv_lora + rope
    row = lambda i: (i, 0)
    return pl.pallas_call(
        functools.partial(_mla_in_kernel, q_lora=q_lora, kv_lora=kv_lora, rope=rope),
        grid=(m // tm,),
        in_specs=[pl.BlockSpec((tm, d), row), _resident((1, d)), _resident(w.shape),
                  _resident((1, q_lora)), _resident((1, kv_lora)),
                  pl.BlockSpec((tm, rope), row), pl.BlockSpec((tm, rope), row)],
        out_specs=[pl.BlockSpec((tm, q_lora), row), pl.BlockSpec((tm, lat), row), pl.BlockSpec((tm, lat), row)],
        out_shape=[jax.ShapeDtypeStruct((m, q_lora), BF16), jax.ShapeDtypeStruct((m, lat), F32),
                   jax.ShapeDtypeStruct((m, lat), BF16)],
        compiler_params=_params("arbitrary"),
        name="mla_in",
    )(h, g, w, qn, kvn, cos2, sin2)


def _mla_q_kernel(cq_ref, wqb_ref, wkt_ref, cos_ref, sin_ref, q_ref, *, heads, nope, rope, kv_lora, scale):
    y = _dot(cq_ref[...], wqb_ref[...])
    cos = cos_ref[...]
    sin = sin_ref[...]
    pe0 = heads * nope
    rot0 = pe0 + heads * rope
    for h in range(heads):
        qn = y[:, h * nope:(h + 1) * nope].astype(BF16)
        q_ref[h, :, :kv_lora] = (_dot(qn, wkt_ref[h]) * scale).astype(BF16)
        pe = y[:, pe0 + h * rope:pe0 + (h + 1) * rope] * cos + y[:, rot0 + h * rope:rot0 + (h + 1) * rope] * sin
        q_ref[h, :, kv_lora:] = (pe * scale).astype(BF16)


def _mla_q(cq, wqb, wkt, cos2, sin2, *, scale):
    m, q_lora = cq.shape
    heads, nope, kv_lora = wkt.shape
    rope = cos2.shape[1]
    tm = min(256, m)
    row = lambda i: (i, 0)
    return pl.pallas_call(
        functools.partial(_mla_q_kernel, heads=heads, nope=nope, rope=rope, kv_lora=kv_lora, scale=scale),
        grid=(m // tm,),
        in_specs=[pl.BlockSpec((tm, q_lora), row), _resident(wqb.shape), _resident(wkt.shape),
                  pl.BlockSpec((tm, rope), row), pl.BlockSpec((tm, rope), row)],
        out_specs=pl.BlockSpec((heads, tm, kv_lora + rope), lambda i: (0, i, 0)),
        out_shape=jax.ShapeDtypeStruct((heads, m, kv_lora + rope), BF16),
        compiler_params=_params("arbitrary"),
        name="mla_q",
    )(cq, wqb, wkt, cos2, sin2)


def _softmax_step(sc, v, m_sc, l_sc, acc_sc):
    m_prev = m_sc[...]
    m_new = jnp.maximum(m_prev, jnp.max(sc, axis=1, keepdims=True))
    alpha = jnp.exp(m_prev - m_new)
    p = jnp.exp(sc - m_new)
    l_sc[...] = alpha * l_sc[...] + jnp.sum(p, axis=1, keepdims=True)
    acc_sc[...] = alpha * acc_sc[...] + _dot(p.astype(BF16), v)
    m_sc[...] = m_new


def _mla_prompt_kernel(qi_ref, kj_ref, q_ref, k_ref, o_ref, m_sc, l_sc, acc_sc, *, heads, tq, tk, kv_lora):
    s = pl.program_id(1)
    qi = qi_ref[s]
    kj = kj_ref[s]
    diag = (qi * tq + tq - 1) // tk
    rows = heads * tq

    @pl.when(kj == 0)
    def _():
        m_sc[...] = jnp.full_like(m_sc, NEG_INF)
        l_sc[...] = jnp.zeros_like(l_sc)
        acc_sc[...] = jnp.zeros_like(acc_sc)

    q = q_ref[...].reshape(rows, q_ref.shape[-1])
    k = k_ref[...]
    sc = _dot_nt(q, k)

    @pl.when(kj != diag)
    def _():
        _softmax_step(sc, k[:, :kv_lora], m_sc, l_sc, acc_sc)

    @pl.when(kj == diag)
    def _():
        q_pos = qi * tq + (lax.broadcasted_iota(jnp.int32, (rows, tk), 0) & (tq - 1))
        k_pos = kj * tk + lax.broadcasted_iota(jnp.int32, (rows, tk), 1)
        _softmax_step(jnp.where(k_pos <= q_pos, sc, NEG_INF), k[:, :kv_lora], m_sc, l_sc, acc_sc)
        o_ref[...] = (acc_sc[...] / l_sc[...]).astype(BF16).reshape(o_ref.shape)


def _mla_prompt_attn(q_full, rows_b, *, batch, seq, kv_lora):
    heads, _, lat = q_full.shape
    tq, tk = MLA_Q_TILE, MLA_K_TILE
    assert tq & (tq - 1) == 0 and seq % tq == 0 and seq % tk == 0
    nq, nk = seq // tq, seq // tk
    qi_list, kj_list = [], []
    for qi in range(nq):
        for kj in range((qi * tq + tq - 1) // tk + 1):
            qi_list.append(qi)
            kj_list.append(kj)
    qi_tab = jnp.asarray(qi_list, jnp.int32)
    kj_tab = jnp.asarray(kj_list, jnp.int32)
    rows = heads * tq
    grid_spec = pltpu.PrefetchScalarGridSpec(
        num_scalar_prefetch=2,
        grid=(batch, len(qi_list)),
        in_specs=[pl.BlockSpec((heads, tq, lat), lambda b, s, qi, kj: (0, b * nq + qi[s], 0)),
                  pl.BlockSpec((tk, lat), lambda b, s, qi, kj: (b * nk + kj[s], 0))],
        out_specs=pl.BlockSpec((heads, tq, kv_lora), lambda b, s, qi, kj: (0, b * nq + qi[s], 0)),
        scratch_shapes=[pltpu.VMEM((rows, 1), F32), pltpu.VMEM((rows, 1), F32), pltpu.VMEM((rows, kv_lora), F32)],
    )
    return pl.pallas_call(
        functools.partial(_mla_prompt_kernel, heads=heads, tq=tq, tk=tk, kv_lora=kv_lora),
        grid_spec=grid_spec,
        out_shape=jax.ShapeDtypeStruct((heads, batch * seq, kv_lora), BF16),
        compiler_params=_params("arbitrary", "arbitrary"),
        name="mla_prompt_attn",
    )(qi_tab, kj_tab, q_full, rows_b)


def _mla_sample_kernel(pt_ref, q_ref, new_ref, *refs, n_pages, dec_seq, kv_lora):
    page_refs = refs[:n_pages]
    o_ref, m_sc, l_sc, acc_sc = refs[n_pages:]
    g = pl.program_id(1)

    @pl.when(g == 0)
    def _():
        m_sc[...] = jnp.full_like(m_sc, NEG_INF)
        l_sc[...] = jnp.zeros_like(l_sc)
        acc_sc[...] = jnp.zeros_like(acc_sc)

    q = q_ref[...]
    k = jnp.concatenate([r[...].astype(BF16) for r in page_refs], axis=0)
    _softmax_step(_dot_nt(q, k), k[:, :kv_lora], m_sc, l_sc, acc_sc)

    @pl.when(g == pl.num_programs(1) - 1)
    def _():
        kn = new_ref[...]
        sc = _dot_nt(q, kn)
        t = lax.broadcasted_iota(jnp.int32, sc.shape, 0) % dec_seq
        c = lax.broadcasted_iota(jnp.int32, sc.shape, 1)
        _softmax_step(jnp.where(c <= t, sc, NEG_INF), kn[:, :kv_lora], m_sc, l_sc, acc_sc)
        o_ref[...] = (acc_sc[...] / l_sc[...]).astype(BF16)


def _mla_sample_attn(q_s, new_pad, cache_mla, layer, page_table, *, dec_seq, kv_lora):
    n_seq, rows, lat = q_s.shape
    n_pages_total = page_table.shape[1]
    p = PAGES_PER_STEP
    assert n_pages_total % p == 0 and cache_mla.shape[2] == PAGE_SIZE
    page_specs = [
        pl.BlockSpec((None, None, PAGE_SIZE, lat), lambda s, g, pt, j=j: (layer, pt[s, g * p + j], 0, 0))
        for j in range(p)
    ]
    grid_spec = pltpu.PrefetchScalarGridSpec(
        num_scalar_prefetch=1,
        grid=(n_seq, n_pages_total // p),
        in_specs=[pl.BlockSpec((None, rows, lat), lambda s, g, pt: (s, 0, 0)),
                  pl.BlockSpec((None, NEW_ROWS_PAD, lat), lambda s, g, pt: (s, 0, 0))] + page_specs,
        out_specs=pl.BlockSpec((None, rows, kv_lora), lambda s, g, pt: (s, 0, 0)),
        scratch_shapes=[pltpu.VMEM((rows, 1), F32), pltpu.VMEM((rows, 1), F32), pltpu.VMEM((rows, kv_lora), F32)],
    )
    return pl.pallas_call(
        functools.partial(_mla_sample_kernel, n_pages=p, dec_seq=dec_seq, kv_lora=kv_lora),
        grid_spec=grid_spec,
        out_shape=jax.ShapeDtypeStruct((n_seq, rows, kv_lora), BF16),
        compiler_params=_params("arbitrary", "arbitrary"),
        name="mla_sample_attn",
    )(page_table, q_s, new_pad, *([cache_mla] * p))


def _mla_out_kernel(o_ref, wv_ref, wo_ref, h_ref, out_ref, *, heads):
    v = jnp.concatenate([_dot(o_ref[h], wv_ref[h]).astype(BF16) for h in range(heads)], axis=1)
    out_ref[...] = h_ref[...] + _dot(v, wo_ref[...])


def _mla_out(o_lat, wv, wo, h):
    heads, m, kv_lora = o_lat.shape
    d = h.shape[1]
    tm = min(256, m)
    row = lambda i: (i, 0)
    return pl.pallas_call(
        functools.partial(_mla_out_kernel, heads=heads),
        grid=(m // tm,),
        in_specs=[pl.BlockSpec((heads, tm, kv_lora), lambda i: (0, i, 0)), _resident(wv.shape), _resident(wo.shape),
                  pl.BlockSpec((tm, d), row)],
        out_specs=pl.BlockSpec((tm, d), row),
        out_shape=jax.ShapeDtypeStruct((m, d), F32),
        compiler_params=_params("arbitrary"),
        name="mla_out",
    )(o_lat, wv, wo, h)


def _ffn_kernel(h_ref, g_ref, wg_ref, wu_ref, wd_ref, fg_ref, out_ref, xn_sc, acc_sc, *, final_norm):
    j = pl.program_id(1)

    @pl.when(j == 0)
    def _():
        xn_sc[...] = _rms(h_ref[...], g_ref[...]).astype(BF16)
        acc_sc[...] = jnp.zeros_like(acc_sc)

    xn = xn_sc[...]
    a = _dot(xn, wg_ref[...])
    u = _dot(xn, wu_ref[...])
    acc_sc[...] += _dot((a * jax.nn.sigmoid(a) * u).astype(BF16), wd_ref[...])

    @pl.when(j == pl.num_programs(1) - 1)
    def _():
        res = h_ref[...] + acc_sc[...]
        out_ref[...] = _rms(res, fg_ref[...]) if final_norm else res


def _ffn(h, g, wg, wu, wd, fg, *, final_norm):
    m, d = h.shape
    dff = wg.shape[1]
    tm = min(ROW_TILE, m)
    tf = FFN_TILE
    assert dff % tf == 0
    row = lambda i, j: (i, 0)
    return pl.pallas_call(
        functools.partial(_ffn_kernel, final_norm=final_norm),
        grid=(m // tm, dff // tf),
        in_specs=[pl.BlockSpec((tm, d), row), _resident((1, d)),
                  pl.BlockSpec((d, tf), lambda i, j: (0, j)), pl.BlockSpec((d, tf), lambda i, j: (0, j)),
                  pl.BlockSpec((tf, d), lambda i, j: (j, 0)), _resident((1, d))],
        out_specs=pl.BlockSpec((tm, d), row),
        out_shape=jax.ShapeDtypeStruct((m, d), F32),
        scratch_shapes=[pltpu.VMEM((tm, d), BF16), pltpu.VMEM((tm, d), F32)],
        compiler_params=_params("arbitrary", "arbitrary"),
        name="ffn",
    )(h, g, wg, wu, wd, fg)


def _swa_in_kernel(h_ref, ga_ref, gk_ref, wq_ref, bq_ref, wkv_ref, bkv_ref, q_ref, kv_ref, kvb_ref, *, scale):
    x = h_ref[...]
    y = x * lax.rsqrt(jnp.mean(x * x, axis=-1, keepdims=True) + EPS)
    xn = (y * ga_ref[...]).astype(BF16)
    hn = (y * gk_ref[...]).astype(BF16)
    q_ref[...] = ((_dot(xn, wq_ref[...]) + bq_ref[...]) * scale).astype(BF16)
    kv = _dot(hn, wkv_ref[...]) + bkv_ref[...]
    kv_ref[...] = kv
    kvb_ref[...] = kv.astype(BF16)


def _swa_in(h, ga, gk, wq, bq, wkv, bkv, *, scale):
    m, d = h.shape
    nq, nkv = wq.shape[1], wkv.shape[1]
    tm = min(256, m)
    row = lambda i: (i, 0)
    return pl.pallas_call(
        functools.partial(_swa_in_kernel, scale=scale),
        grid=(m // tm,),
        in_specs=[pl.BlockSpec((tm, d), row), _resident((1, d)), _resident((1, d)), _resident(wq.shape),
                  _resident((1, nq)), _resident(wkv.shape), _resident((1, nkv))],
        out_specs=[pl.BlockSpec((tm, nq), row), pl.BlockSpec((tm, nkv), row), pl.BlockSpec((tm, nkv), row)],
        out_shape=[jax.ShapeDtypeStruct((m, nq), BF16), jax.ShapeDtypeStruct((m, nkv), F32),
                   jax.ShapeDtypeStruct((m, nkv), BF16)],
        compiler_params=_params("arbitrary"),
        name="swa_in",
    )(h, ga, gk, wq, bq, wkv, bkv)


def _sink_softmax_pv(s, sink, v):
    m = jnp.maximum(jnp.max(s, axis=1, keepdims=True), sink)
    p = jnp.exp(s - m)
    denom = jnp.sum(p, axis=1, keepdims=True) + jnp.exp(sink - m)
    return _dot(p.astype(BF16), v) / denom


def _swa_prompt_kernel(q_ref, prev_ref, cur_ref, bias_ref, sink_ref, o_ref, *, n_kv, group, hd):
    i = pl.program_id(1)
    w = q_ref.shape[0]
    q = q_ref[...]
    kv = jnp.concatenate([prev_ref[...], cur_ref[...]], axis=0)
    col = lax.broadcasted_iota(jnp.int32, (1, 2 * w), 1)
    in_seq = jnp.logical_or(col >= w, i > 0)
    for n in range(n_kv):
        k = kv[:, n * hd:(n + 1) * hd]
        v = kv[:, (n_kv + n) * hd:(n_kv + n + 1) * hd]
        qn = jnp.concatenate([q[:, (n * group + g) * hd:(n * group + g + 1) * hd] for g in range(group)], axis=0)
        s = _dot_nt(qn, k) + bias_ref[n * group:(n + 1) * group].reshape(group * w, 2 * w)
        o = _sink_softmax_pv(jnp.where(in_seq, s, NEG_INF), sink_ref[n], v).astype(BF16)
        for g in range(group):
            o_ref[:, (n * group + g) * hd:(n * group + g + 1) * hd] = o[g * w:(g + 1) * w]


def _swa_prompt_attn(q, kvb, bias, sink_rows, *, batch, seq, n_kv, group, hd):
    w = WINDOW
    nb = seq // w
    return pl.pallas_call(
        functools.partial(_swa_prompt_kernel, n_kv=n_kv, group=group, hd=hd),
        grid=(batch, nb),
        in_specs=[pl.BlockSpec((w, q.shape[1]), lambda b, i: (b * nb + i, 0)),
                  pl.BlockSpec((w, kvb.shape[1]), lambda b, i: (b * nb + jnp.maximum(i - 1, 0), 0)),
                  pl.BlockSpec((w, kvb.shape[1]), lambda b, i: (b * nb + i, 0)),
                  _resident(bias.shape), _resident(sink_rows.shape)],
        out_specs=pl.BlockSpec((w, q.shape[1]), lambda b, i: (b * nb + i, 0)),
        out_shape=jax.ShapeDtypeStruct(q.shape, BF16),
        compiler_params=_params("arbitrary", "arbitrary"),
        name="swa_prompt_attn",
    )(q, kvb, kvb, bias, sink_rows)


def _swa_sample_kernel(q_ref, state_ref, new_ref, bias_ref, sink_ref, o_ref, win_ref, *, n_kv, hd, dec_seq):
    n_win = state_ref.shape[1]
    for sb in range(q_ref.shape[0]):
        state = state_ref[sb]
        new = new_ref[sb]
        win_ref[sb, :n_win - dec_seq] = state[dec_seq:]
        win_ref[sb, n_win - dec_seq:] = new[:dec_seq]
        kv = jnp.concatenate([state, new], axis=0).astype(BF16)
        for n in range(n_kv):
            k = kv[:, n * hd:(n + 1) * hd]
            v = kv[:, (n_kv + n) * hd:(n_kv + n + 1) * hd]
            s = _dot_nt(q_ref[sb, n], k) + bias_ref[n]
            o_ref[sb, n] = _sink_softmax_pv(s, sink_ref[n], v).astype(BF16)


def _swa_sample_attn(q4, state, new_pad, bias, sink_rows, *, dec_seq):
    n_seq, n_kv, rows, hd = q4.shape
    n_win, width = state.shape[1], state.shape[2]
    sb = SWA_SEQS_PER_STEP
    assert n_seq % sb == 0
    blk = lambda i: (i, 0, 0)
    return pl.pallas_call(
        functools.partial(_swa_sample_kernel, n_kv=n_kv, hd=hd, dec_seq=dec_seq),
        grid=(n_seq // sb,),
        in_specs=[pl.BlockSpec((sb, n_kv, rows, hd), lambda i: (i, 0, 0, 0)),
                  pl.BlockSpec((sb, n_win, width), blk), pl.BlockSpec((sb,) + new_pad.shape[1:], blk),
                  _resident(bias.shape), _resident(sink_rows.shape)],
        out_specs=[pl.BlockSpec((sb, n_kv, rows, hd), lambda i: (i, 0, 0, 0)), pl.BlockSpec((sb, n_win, width), blk)],
        out_shape=[jax.ShapeDtypeStruct(q4.shape, BF16), jax.ShapeDtypeStruct(state.shape, F32)],
        compiler_params=_params("arbitrary"),
        name="swa_sample_attn",
    )(q4, state, new_pad, bias, sink_rows)


def _proj_residual_kernel(o_ref, w_ref, b_ref, h_ref, out_ref):
    out_ref[...] = h_ref[...] + (_dot(o_ref[...], w_ref[...]) + b_ref[...])


def _proj_residual(o, w, b, h):
    m, d = h.shape
    tm = min(ROW_TILE, m)
    row = lambda i: (i, 0)
    return pl.pallas_call(
        _proj_residual_kernel,
        grid=(m // tm,),
        in_specs=[pl.BlockSpec((tm, o.shape[1]), row), _resident(w.shape), _resident((1, d)), pl.BlockSpec((tm, d), row)],
        out_specs=pl.BlockSpec((tm, d), row),
        out_shape=jax.ShapeDtypeStruct((m, d), F32),
        compiler_params=_params("arbitrary"),
        name="swa_out",
    )(o, w, b, h)


def _rope_tables(pos, rope):
    inv_freq = ROPE_THETA ** (-jnp.arange(0, rope, 2, dtype=F32) / rope)
    ang = pos.astype(F32)[:, None] * inv_freq[None, :]
    cos, sin = jnp.cos(ang), jnp.sin(ang)
    return jnp.concatenate([cos, cos], axis=1), jnp.concatenate([sin, sin], axis=1)


def _rot_half_cols(w):
    half = w.shape[-1] // 2
    return jnp.concatenate([-w[..., half:], w[..., :half]], axis=-1)


def _bucket(dist):
    n = jnp.maximum(dist, 0)
    max_exact = N_BUCKETS // 2
    nf = jnp.maximum(n, 1).astype(F32)
    large = max_exact + (jnp.log(nf / max_exact) / math.log(MAX_DISTANCE / max_exact)
                         * (N_BUCKETS - max_exact)).astype(jnp.int32)
    return jnp.where(n < max_exact, n, jnp.minimum(large, N_BUCKETS - 1))


def _window_bias(rel_bias, dist):
    valid = (dist >= 0) & (dist < WINDOW)
    bias = jnp.moveaxis(jnp.take(rel_bias.astype(F32), _bucket(dist), axis=0), -1, 0)
    return jnp.where(valid[None], bias, NEG_INF)


def kernel(x_prompt, x_sample, cache_mla, state_kv_win, page_table, attn_norm, ffn_norm, final_norm, mla_wq_a, mla_q_norm, mla_wq_b, mla_wkv_a, mla_kv_norm, mla_wkv_b, mla_wo, kv_norm_shared, w_k_shared, b_k_shared, w_v_shared, b_v_shared, swa_wq, swa_bq, swa_sinks, swa_wo, swa_bo, rel_bias, ffn_w_gate, ffn_w_up, ffn_w_down):
    batch, seq, d = x_prompt.shape
    dec_batch, dec_seq, _ = x_sample.shape
    depth = attn_norm.shape[0]
    n_a = mla_wq_a.shape[0]
    past_len = page_table.shape[1] * PAGE_SIZE
    n_win = state_kv_win.shape[1]
    q_lora = mla_wq_a.shape[2]
    kv_lora = mla_kv_norm.shape[1]
    lat = mla_wkv_a.shape[2]
    rope = lat - kv_lora
    heads_a = mla_wq_b.shape[2]
    nope = mla_wq_b.shape[3] - rope
    heads_b, hd = swa_wq.shape[2], swa_wq.shape[3]
    n_kv = w_k_shared.shape[1]
    group = heads_b // n_kv
    assert seq % WINDOW == 0 and n_win == WINDOW and dec_seq <= 8
    mla_scale = 1.0 / math.sqrt(nope + rope)
    swa_scale = 1.0 / math.sqrt(hd)

    vec = lambda a: a.reshape(1, -1).astype(F32)
    streams = [x_prompt.reshape(batch * seq, d), x_sample.reshape(dec_batch * dec_seq, d)]
    rope_tabs = [_rope_tables(jnp.tile(jnp.arange(seq, dtype=jnp.int32), batch), rope),
                 _rope_tables(jnp.tile(past_len + jnp.arange(dec_seq, dtype=jnp.int32), dec_batch), rope)]

    rows_p_all, rows_s_all = [], []
    kv_p = kv_win_sample = None
    for layer in range(depth):
        ga = vec(attn_norm[layer])
        if layer < n_a:
            a = layer
            wkv_a = mla_wkv_a[a]
            w_in = jnp.concatenate([mla_wq_a[a], wkv_a, _rot_half_cols(wkv_a[:, kv_lora:])], axis=1).astype(BF16)
            wq_pe = mla_wq_b[a][:, :, nope:]
            wqb = jnp.concatenate([mla_wq_b[a][:, :, :nope].reshape(q_lora, heads_a * nope),
                                   wq_pe.reshape(q_lora, heads_a * rope),
                                   _rot_half_cols(wq_pe).reshape(q_lora, heads_a * rope)], axis=1).astype(BF16)
            wkt = jnp.transpose(mla_wkv_b[a][:, :, :nope], (1, 2, 0)).astype(BF16)
            wv = jnp.transpose(mla_wkv_b[a][:, :, nope:], (1, 0, 2)).astype(BF16)
            wo = mla_wo[a].reshape(-1, d).astype(BF16)

            stage = []
            for h, (cos2, sin2) in zip(streams, rope_tabs):
                cq, rows, rows_b = _mla_in(h, ga, w_in, vec(mla_q_norm[a]), vec(mla_kv_norm[a]), cos2, sin2,
                                           q_lora=q_lora, kv_lora=kv_lora, rope=rope)
                stage.append((_mla_q(cq, wqb, wkt, cos2, sin2, scale=mla_scale), rows, rows_b))
            (q_p, rows_p, rows_pb), (q_s, rows_s, rows_sb) = stage
            rows_p_all.append(rows_p.reshape(batch, seq, lat))
            rows_s_all.append(rows_s.reshape(dec_batch, dec_seq, lat))

            o_p = _mla_prompt_attn(q_p, rows_pb, batch=batch, seq=seq, kv_lora=kv_lora)
            q_s = q_s.reshape(heads_a, dec_batch, dec_seq, lat).transpose(1, 0, 2, 3).reshape(dec_batch, heads_a * dec_seq, lat)
            new_pad = jnp.pad(rows_sb.reshape(dec_batch, dec_seq, lat), ((0, 0), (0, NEW_ROWS_PAD - dec_seq), (0, 0)))
            o_s = _mla_sample_attn(q_s, new_pad, cache_mla, a, page_table, dec_seq=dec_seq, kv_lora=kv_lora)
            o_s = o_s.reshape(dec_batch, heads_a, dec_seq, kv_lora).transpose(1, 0, 2, 3).reshape(heads_a, dec_batch * dec_seq, kv_lora)
            streams = [_mla_out(o, wv, wo, h) for o, h in zip((o_p, o_s), streams)]
        else:
            b = layer - n_a
            wq = swa_wq[b].reshape(d, heads_b * hd).astype(BF16)
            wkv = jnp.concatenate([w_k_shared.reshape(d, n_kv * hd), w_v_shared.reshape(d, n_kv * hd)], axis=1).astype(BF16)
            bkv = jnp.concatenate([b_k_shared.reshape(1, -1), b_v_shared.reshape(1, -1)], axis=1).astype(F32)
            (q_p, kv_p_new, kvb_p), (q_s, kv_s_new, _) = [
                _swa_in(h, ga, vec(kv_norm_shared), wq, vec(swa_bq[b]), wkv, bkv, scale=swa_scale) for h in streams]
            if b == 0:
                kv_p, kv_pb, kv_s = kv_p_new, kvb_p, kv_s_new
                state = state_kv_win.reshape(dec_batch, n_win, 2 * n_kv * hd)
            sinks = swa_sinks[b].astype(F32)

            a_idx = jnp.arange(WINDOW, dtype=jnp.int32)[:, None]
            c_idx = jnp.arange(2 * WINDOW, dtype=jnp.int32)[None, :]
            bias_p = _window_bias(rel_bias, a_idx + WINDOW - c_idx)
            sink_p = jnp.repeat(sinks, WINDOW).reshape(n_kv, group * WINDOW, 1)
            o_p = _swa_prompt_attn(q_p, kv_pb, bias_p, sink_p, batch=batch, seq=seq, n_kv=n_kv, group=group, hd=hd)

            n_keys = n_win + 8
            t_idx = jnp.arange(dec_seq, dtype=jnp.int32)[:, None]
            c_idx = jnp.arange(n_keys, dtype=jnp.int32)[None, :]
            dist_s = jnp.where(c_idx < n_win + dec_seq, n_win + t_idx - c_idx, -1)
            bias_s = _window_bias(rel_bias, dist_s)
            bias_s = bias_s.reshape(n_kv, group * dec_seq, n_keys)
            sink_s = jnp.repeat(sinks, dec_seq).reshape(n_kv, group * dec_seq, 1)
            q4 = q_s.reshape(dec_batch, dec_seq, n_kv, group, hd).transpose(0, 2, 3, 1, 4).reshape(dec_batch, n_kv, group * dec_seq, hd)
            new_pad = jnp.pad(kv_s.reshape(dec_batch, dec_seq, -1), ((0, 0), (0, 8 - dec_seq), (0, 0)))
            o4, win = _swa_sample_attn(q4, state, new_pad, bias_s, sink_s, dec_seq=dec_seq)
            if b == 0:
                kv_win_sample = win
            o_s = o4.reshape(dec_batch, n_kv, group, dec_seq, hd).transpose(0, 3, 1, 2, 4).reshape(dec_batch * dec_seq, heads_b * hd)
            wo = swa_wo[b].reshape(heads_b * hd, d).astype(BF16)
            streams = [_proj_residual(o, wo, vec(swa_bo[b]), h) for o, h in zip((o_p, o_s), streams)]

        last = layer == depth - 1
        streams = [_ffn(h, vec(ffn_norm[layer]), ffn_w_gate[layer].astype(BF16), ffn_w_up[layer].astype(BF16),
                        ffn_w_down[layer].astype(BF16), vec(final_norm), final_norm=last) for h in streams]

    y_prompt = streams[0].reshape(batch, seq, d)
    y_sample = streams[1].reshape(dec_batch, dec_seq, d)
    kv_win_prompt = kv_p.reshape(batch, seq, 2, n_kv, hd)[:, seq - min(WINDOW, seq):]
    kv_win_sample = kv_win_sample.reshape(dec_batch, n_win, 2, n_kv, hd)
    return (y_prompt, y_sample, jnp.stack(rows_p_all, axis=0), jnp.stack(rows_s_all, axis=0),
            kv_win_prompt, kv_win_sample)
```

```python
import functools
import math

import jax
import jax.numpy as jnp
from jax import lax
from jax.experimental import pallas as pl
from jax.experimental.pallas import tpu as pltpu

F32 = jnp.float32
BF16 = jnp.bfloat16

EPS = 1e-6
NEG_INF = -1e30
ROPE_THETA = 10000.0
PAGE_SIZE = 128
WINDOW = 128
N_BUCKETS = 32
MAX_DISTANCE = 128

VMEM_LIMIT_BYTES = 52 * 1024 * 1024
ROW_TILE = 512
FFN_TILE = 512
MLA_Q_TILE = 128
MLA_K_TILE = 512
MLA_HEADS_PER_PASS = 4
PAGES_PER_STEP = 16
NEW_ROWS_PAD = 16
SWA_SEQS_PER_STEP = 8


def _params(*semantics):
    return pltpu.CompilerParams(dimension_semantics=semantics, vmem_limit_bytes=VMEM_LIMIT_BYTES)


def _rms(x, g):
    return x * lax.rsqrt(jnp.mean(x * x, axis=-1, keepdims=True) + EPS) * g


def _dot(a, b):
    return jnp.dot(a, b, preferred_element_type=F32)


def _dot_nt(a, b):
    return lax.dot_general(a, b, (((1,), (1,)), ((), ())), preferred_element_type=F32)


def _resident(shape):
    return pl.BlockSpec(shape, lambda *_: (0,) * len(shape))


def _mla_in_kernel(h_ref, g_ref, w_ref, qn_ref, kvn_ref, cos_ref, sin_ref,
                   cq_ref, rows_ref, rowsb_ref, *, q_lora, kv_lora, rope):
    xn = _rms(h_ref[...], g_ref[...]).astype(BF16)
    y = _dot(xn, w_ref[...])
    cq_ref[...] = _rms(y[:, :q_lora], qn_ref[...]).astype(BF16)
    c0 = q_lora
    ckv = _rms(y[:, c0:c0 + kv_lora], kvn_ref[...])
    p0 = c0 + kv_lora
    kpe = y[:, p0:p0 + rope] * cos_ref[...] + y[:, p0 + rope:p0 + 2 * rope] * sin_ref[...]
    rows_ref[:, :kv_lora] = ckv
    rows_ref[:, kv_lora:] = kpe
    rowsb_ref[:, :kv_lora] = ckv.astype(BF16)
    rowsb_ref[:, kv_lora:] = kpe.astype(BF16)


def _mla_in(h, g, w, qn, kvn, cos2, sin2, *, q_lora, kv_lora, rope):
    m, d = h.shape
    tm = min(ROW_TILE, m)
    lat = kv_lora + rope
    row = lambda i: (i, 0)
    return pl.pallas_call(
        functools.partial(_mla_in_kernel, q_lora=q_lora, kv_lora=kv_lora, rope=rope),
        grid=(m // tm,),
        in_specs=[pl.BlockSpec((tm, d), row), _resident((1, d)), _resident(w.shape),
                  _resident((1, q_lora)), _resident((1, kv_lora)),
                  pl.BlockSpec((tm, rope), row), pl.BlockSpec((tm, rope), row)],
        out_specs=[pl.BlockSpec((tm, q_lora), row), pl.BlockSpec((tm, lat), row), pl.BlockSpec((tm, lat), row)],
        out_shape=[jax.ShapeDtypeStruct((m, q_lora), BF16), jax.ShapeDtypeStruct((m, lat), F32),
                   jax.ShapeDtypeStruct((m, lat), BF16)],
        compiler_params=_params("arbitrary"),
        name="mla_in",
    )(h, g, w, qn, kvn, cos2, sin2)


def _mla_q_kernel(cq_ref, wqb_ref, wkt_ref, cos_ref, sin_ref, q_ref, *, heads, nope, rope, kv_lora, scale):
    y = _dot(cq_ref[...], wqb_ref[...])
    cos = cos_ref[...]
    sin = sin_ref[...]
    pe0 = heads * nope
    rot0 = pe0 + heads * rope
    for h in range(heads):
        qn = y[:, h * nope:(h + 1) * nope].astype(BF16)
        q_ref[h, :, :kv_lora] = (_dot(qn, wkt_ref[h]) * scale).astype(BF16)
        pe = y[:, pe0 + h * rope:pe0 + (h + 1) * rope] * cos + y[:, rot0 + h * rope:rot0 + (h + 1) * rope] * sin
        q_ref[h, :, kv_lora:] = (pe * scale).astype(BF16)


def _mla_q(cq, wqb, wkt, cos2, sin2, *, scale):
    m, q_lora = cq.shape
    heads, nope, kv_lora = wkt.shape
    rope = cos2.shape[1]
    tm = min(256, m)
    row = lambda i: (i, 0)
    return pl.pallas_call(
        functools.partial(_mla_q_kernel, heads=heads, nope=nope, rope=rope, kv_lora=kv_lora, scale=scale),
        grid=(m // tm,),
        in_specs=[pl.BlockSpec((tm, q_lora), row), _resident(wqb.shape), _resident(wkt.shape),
                  pl.BlockSpec((tm, rope), row), pl.BlockSpec((tm, rope), row)],
        out_specs=pl.BlockSpec((heads, tm, kv_lora + rope), lambda i: (0, i, 0)),
        out_shape=jax.ShapeDtypeStruct((heads, m, kv_lora + rope), BF16),
        compiler_params=_params("arbitrary"),
        name="mla_q",
    )(cq, wqb, wkt, cos2, sin2)


def _softmax_step(sc, pv, m_sc, l_sc, acc_sc, r=slice(None)):
    m_prev = m_sc[r]
    m_new = jnp.maximum(m_prev, jnp.max(sc, axis=1, keepdims=True))
    alpha = jnp.exp(m_prev - m_new)
    p = jnp.exp(sc - m_new)
    l_sc[r] = alpha * l_sc[r] + jnp.sum(p, axis=1, keepdims=True)
    acc_sc[r] = alpha * acc_sc[r] + pv(p.astype(BF16))
    m_sc[r] = m_new


def _mla_prompt_kernel(qi_ref, kj_ref, q_ref, k_ref, o_ref, m_sc, l_sc, acc_sc, *, heads, tq, tk, kv_lora):
    s = pl.program_id(1)
    qi = qi_ref[s]
    kj = kj_ref[s]
    diag = (qi * tq + tq - 1) // tk
    hp = MLA_HEADS_PER_PASS
    rows = hp * tq

    @pl.when(kj == 0)
    def _():
        m_sc[...] = jnp.full_like(m_sc, NEG_INF)
        l_sc[...] = jnp.zeros_like(l_sc)
        acc_sc[...] = jnp.zeros_like(acc_sc)

    def step(mask):
        k = k_ref[...]
        v = k[:, :kv_lora]
        for g in range(heads // hp):
            sc = _dot_nt(q_ref[g * hp:(g + 1) * hp].reshape(rows, q_ref.shape[-1]), k)
            if mask is not None:
                sc = jnp.where(mask, sc, NEG_INF)
            _softmax_step(sc, lambda p: _dot(p, v), m_sc, l_sc, acc_sc, slice(g * rows, (g + 1) * rows))

    @pl.when(kj != diag)
    def _():
        step(None)

    @pl.when(kj == diag)
    def _():
        q_pos = qi * tq + (lax.broadcasted_iota(jnp.int32, (rows, tk), 0) & (tq - 1))
        k_pos = kj * tk + lax.broadcasted_iota(jnp.int32, (rows, tk), 1)
        step(k_pos <= q_pos)
        o_ref[...] = (acc_sc[...] / l_sc[...]).astype(BF16).reshape(o_ref.shape)


def _mla_prompt_attn(q_full, rows_b, *, batch, seq, kv_lora):
    heads, _, lat = q_full.shape
    tq, tk = MLA_Q_TILE, MLA_K_TILE
    assert tq & (tq - 1) == 0 and seq % tq == 0 and seq % tk == 0
    nq, nk = seq // tq, seq // tk
    qi_list, kj_list = [], []
    for qi in range(nq):
        for kj in range((qi * tq + tq - 1) // tk + 1):
            qi_list.append(qi)
            kj_list.append(kj)
    qi_tab = jnp.asarray(qi_list, jnp.int32)
    kj_tab = jnp.asarray(kj_list, jnp.int32)
    rows = heads * tq
    grid_spec = pltpu.PrefetchScalarGridSpec(
        num_scalar_prefetch=2,
        grid=(batch, len(qi_list)),
        in_specs=[pl.BlockSpec((heads, tq, lat), lambda b, s, qi, kj: (0, b * nq + qi[s], 0)),
                  pl.BlockSpec((tk, lat), lambda b, s, qi, kj: (b * nk + kj[s], 0))],
        out_specs=pl.BlockSpec((heads, tq, kv_lora), lambda b, s, qi, kj: (0, b * nq + qi[s], 0)),
        scratch_shapes=[pltpu.VMEM((rows, 1), F32), pltpu.VMEM((rows, 1), F32), pltpu.VMEM((rows, kv_lora), F32)],
    )
    return pl.pallas_call(
        functools.partial(_mla_prompt_kernel, heads=heads, tq=tq, tk=tk, kv_lora=kv_lora),
        grid_spec=grid_spec,
        out_shape=jax.ShapeDtypeStruct((heads, batch * seq, kv_lora), BF16),
        compiler_params=_params("arbitrary", "arbitrary"),
        name="mla_prompt_attn",
    )(qi_tab, kj_tab, q_full, rows_b)


def _mla_sample_kernel(pt_ref, q_ref, new_ref, *refs, n_pages, dec_seq, kv_lora):
    page_refs = refs[:n_pages]
    o_ref, m_sc, l_sc, acc_sc = refs[n_pages:]
    g = pl.program_id(1)

    @pl.when(g == 0)
    def _():
        m_sc[...] = jnp.full_like(m_sc, NEG_INF)
        l_sc[...] = jnp.zeros_like(l_sc)
        acc_sc[...] = jnp.zeros_like(acc_sc)

    q = q_ref[...]
    kt = jnp.concatenate([r[...].astype(BF16) for r in page_refs], axis=1)
    vt = kt[:kv_lora]
    _softmax_step(_dot(q, kt), lambda p: _dot_nt(p, vt), m_sc, l_sc, acc_sc)

    @pl.when(g == pl.num_programs(1) - 1)
    def _():
        kn = new_ref[...]
        sc = _dot_nt(q, kn)
        t = lax.broadcasted_iota(jnp.int32, sc.shape, 0) % dec_seq
        c = lax.broadcasted_iota(jnp.int32, sc.shape, 1)
        _softmax_step(jnp.where(c <= t, sc, NEG_INF), lambda p: _dot(p, kn[:, :kv_lora]), m_sc, l_sc, acc_sc)
        o_ref[...] = (acc_sc[...] / l_sc[...]).astype(BF16)


def _mla_sample_attn(q_s, new_pad, cache_t, layer, page_table, *, dec_seq, kv_lora):
    n_seq, rows, lat = q_s.shape
    n_pages_total = page_table.shape[1]
    p = PAGES_PER_STEP
    assert n_pages_total % p == 0 and cache_t.shape[2:] == (lat, PAGE_SIZE)
    page_specs = [
        pl.BlockSpec((None, None, lat, PAGE_SIZE), lambda s, g, pt, j=j: (layer, pt[s, g * p + j], 0, 0))
        for j in range(p)
    ]
    grid_spec = pltpu.PrefetchScalarGridSpec(
        num_scalar_prefetch=1,
        grid=(n_seq, n_pages_total // p),
        in_specs=[pl.BlockSpec((None, rows, lat), lambda s, g, pt: (s, 0, 0)),
                  pl.BlockSpec((None, NEW_ROWS_PAD, lat), lambda s, g, pt: (s, 0, 0))] + page_specs,
        out_specs=pl.BlockSpec((None, rows, kv_lora), lambda s, g, pt: (s, 0, 0)),
        scratch_shapes=[pltpu.VMEM((rows, 1), F32), pltpu.VMEM((rows, 1), F32), pltpu.VMEM((rows, kv_lora), F32)],
    )
    return pl.pallas_call(
        functools.partial(_mla_sample_kernel, n_pages=p, dec_seq=dec_seq, kv_lora=kv_lora),
        grid_spec=grid_spec,
        out_shape=jax.ShapeDtypeStruct((n_seq, rows, kv_lora), BF16),
        compiler_params=_params("arbitrary", "arbitrary"),
        name="mla_sample_attn",
    )(page_table, q_s, new_pad, *([cache_t] * p))


def _mla_out_kernel(o_ref, wv_ref, wo_ref, h_ref, out_ref, *, heads):
    v = jnp.concatenate([_dot(o_ref[h], wv_ref[h]).astype(BF16) for h in range(heads)], axis=1)
    out_ref[...] = h_ref[...] + _dot(v, wo_ref[...])


def _mla_out(o_lat, wv, wo, h):
    heads, m, kv_lora = o_lat.shape
    d = h.shape[1]
    tm = min(256, m)
    row = lambda i: (i, 0)
    return pl.pallas_call(
        functools.partial(_mla_out_kernel, heads=heads),
        grid=(m // tm,),
        in_specs=[pl.BlockSpec((heads, tm, kv_lora), lambda i: (0, i, 0)), _resident(wv.shape), _resident(wo.shape),
                  pl.BlockSpec((tm, d), row)],
        out_specs=pl.BlockSpec((tm, d), row),
        out_shape=jax.ShapeDtypeStruct((m, d), F32),
        compiler_params=_params("arbitrary"),
        name="mla_out",
    )(o_lat, wv, wo, h)


def _ffn_kernel(h_ref, g_ref, wg_ref, wu_ref, wd_ref, fg_ref, out_ref, xn_sc, acc_sc, *, final_norm):
    j = pl.program_id(1)

    @pl.when(j == 0)
    def _():
        xn_sc[...] = _rms(h_ref[...], g_ref[...]).astype(BF16)
        acc_sc[...] = jnp.zeros_like(acc_sc)

    xn = xn_sc[...]
    a = _dot(xn, wg_ref[...])
    u = _dot(xn, wu_ref[...])
    acc_sc[...] += _dot((a * jax.nn.sigmoid(a) * u).astype(BF16), wd_ref[...])

    @pl.when(j == pl.num_programs(1) - 1)
    def _():
        res = h_ref[...] + acc_sc[...]
        out_ref[...] = _rms(res, fg_ref[...]) if final_norm else res


def _ffn(h, g, wg, wu, wd, fg, layer, *, final_norm):
    m, d = h.shape
    dff = wg.shape[2]
    tm = min(ROW_TILE, m)
    tf = FFN_TILE
    assert dff % tf == 0
    row = lambda i, j: (i, 0)
    return pl.pallas_call(
        functools.partial(_ffn_kernel, final_norm=final_norm),
        grid=(m // tm, dff // tf),
        in_specs=[pl.BlockSpec((tm, d), row), _resident((1, d)),
                  pl.BlockSpec((None, d, tf), lambda i, j: (layer, 0, j)),
                  pl.BlockSpec((None, d, tf), lambda i, j: (layer, 0, j)),
                  pl.BlockSpec((None, tf, d), lambda i, j: (layer, j, 0)), _resident((1, d))],
        out_specs=pl.BlockSpec((tm, d), row),
        out_shape=jax.ShapeDtypeStruct((m, d), F32),
        scratch_shapes=[pltpu.VMEM((tm, d), BF16), pltpu.VMEM((tm, d), F32)],
        compiler_params=_params("arbitrary", "arbitrary"),
        name="ffn",
    )(h, g, wg, wu, wd, fg)


def _swa_in_kernel(h_ref, ga_ref, gk_ref, wq_ref, bq_ref, wkv_ref, bkv_ref, q_ref, kv_ref, kvb_ref, *, scale):
    x = h_ref[...]
    y = x * lax.rsqrt(jnp.mean(x * x, axis=-1, keepdims=True) + EPS)
    xn = (y * ga_ref[...]).astype(BF16)
    hn = (y * gk_ref[...]).astype(BF16)
    q_ref[...] = ((_dot(xn, wq_ref[...]) + bq_ref[...]) * scale).astype(BF16)
    kv = _dot(hn, wkv_ref[...]) + bkv_ref[...]
    kv_ref[...] = kv[:, :kv_ref.shape[1]]
    kvb_ref[...] = kv.astype(BF16)


def _swa_in(h, ga, gk, wq, bq, wkv, bkv, *, scale):
    m, d = h.shape
    nq, nkv = wq.shape[1], wkv.shape[1]
    tm = min(256, m)
    row = lambda i: (i, 0)
    return pl.pallas_call(
        functools.partial(_swa_in_kernel, scale=scale),
        grid=(m // tm,),
        in_specs=[pl.BlockSpec((tm, d), row), _resident((1, d)), _resident((1, d)), _resident(wq.shape),
                  _resident((1, nq)), _resident(wkv.shape), _resident((1, nkv))],
        out_specs=[pl.BlockSpec((tm, nq), row), pl.BlockSpec((tm, nkv // 2), row), pl.BlockSpec((tm, nkv), row)],
        out_shape=[jax.ShapeDtypeStruct((m, nq), BF16), jax.ShapeDtypeStruct((m, nkv // 2), F32),
                   jax.ShapeDtypeStruct((m, nkv), BF16)],
        compiler_params=_params("arbitrary"),
        name="swa_in",
    )(h, ga, gk, wq, bq, wkv, bkv)


def _sink_softmax_pv(s, sink, v):
    m = jnp.maximum(jnp.max(s, axis=1, keepdims=True), sink)
    p = jnp.exp(s - m)
    denom = jnp.sum(p, axis=1, keepdims=True) + jnp.exp(sink - m)
    return _dot(p.astype(BF16), v) / denom


def _swa_prompt_kernel(q_ref, prev_ref, cur_ref, bias_ref, sink_ref, o_ref, *, n_kv, group, hd):
    i = pl.program_id(1)
    w = q_ref.shape[0]
    lanes = 2 * hd
    pairs = group // 2
    band = 2 * w
    q = q_ref[...]
    kv = jnp.concatenate([prev_ref[...], cur_ref[...]], axis=0)
    col = lax.broadcasted_iota(jnp.int32, (1, 2 * band), 1)
    in_seq = jnp.logical_or((col & (band - 1)) >= w, i > 0)
    low = lax.broadcasted_iota(jnp.int32, (1, lanes), 1) < hd
    zero = jnp.zeros((), BF16)
    swapped = n_kv * lanes

    def block_diag(slab0, n):
        nat = kv[:, slab0 + (n // 2) * lanes:slab0 + (n // 2 + 1) * lanes]
        swp = kv[:, swapped + slab0 + (n // 2) * lanes:swapped + slab0 + (n // 2 + 1) * lanes]
        first, second = (nat, swp) if n % 2 == 0 else (swp, nat)
        return jnp.concatenate([jnp.where(low, first, zero), jnp.where(low, zero, second)], axis=0)

    for n in range(n_kv):
        k2 = block_diag(0, n)
        v2 = block_diag(n_kv * hd, n)
        qn = jnp.concatenate([q[:, (n * pairs + p) * lanes:(n * pairs + p + 1) * lanes] for p in range(pairs)], axis=0)
        s = jnp.where(in_seq, _dot_nt(qn, k2) + bias_ref[n], NEG_INF)
        probs, recips = [], []
        for half in range(2):
            sh = s[:, half * band:(half + 1) * band]
            sink = sink_ref[n, half]
            m = jnp.maximum(jnp.max(sh, axis=1, keepdims=True), sink)
            p = jnp.exp(sh - m)
            probs.append(p.astype(BF16))
            recips.append(1.0 / (jnp.sum(p, axis=1, keepdims=True) + jnp.exp(sink - m)))
        o = _dot(jnp.concatenate(probs, axis=1), v2) * jnp.where(low, recips[0], recips[1])
        for p in range(pairs):
            o_ref[:, (n * pairs + p) * lanes:(n * pairs + p + 1) * lanes] = o[p * w:(p + 1) * w].astype(BF16)


def _swa_prompt_attn(q, kvb, bias, sink_rows, *, batch, seq, n_kv, group, hd):
    w = WINDOW
    nb = seq // w
    assert group % 2 == 0 and kvb.shape[1] == 4 * n_kv * hd
    return pl.pallas_call(
        functools.partial(_swa_prompt_kernel, n_kv=n_kv, group=group, hd=hd),
        grid=(batch, nb),
        in_specs=[pl.BlockSpec((w, q.shape[1]), lambda b, i: (b * nb + i, 0)),
                  pl.BlockSpec((w, kvb.shape[1]), lambda b, i: (b * nb + jnp.maximum(i - 1, 0), 0)),
                  pl.BlockSpec((w, kvb.shape[1]), lambda b, i: (b * nb + i, 0)),
                  _resident(bias.shape), _resident(sink_rows.shape)],
        out_specs=pl.BlockSpec((w, q.shape[1]), lambda b, i: (b * nb + i, 0)),
        out_shape=jax.ShapeDtypeStruct(q.shape, BF16),
        compiler_params=_params("arbitrary", "arbitrary"),
        name="swa_prompt_attn",
    )(q, kvb, kvb, bias, sink_rows)


def _swa_sample_kernel(q_ref, state_ref, new_ref, bias_ref, sink_ref, o_ref, win_ref, *, n_kv, hd, dec_seq):
    n_win = state_ref.shape[1]
    for sb in range(q_ref.shape[0]):
        state = state_ref[sb]
        new = new_ref[sb]
        win_ref[sb, :n_win - dec_seq] = state[dec_seq:]
        win_ref[sb, n_win - dec_seq:] = new[:dec_seq]
        kv = jnp.concatenate([state, new], axis=0).astype(BF16)
        for n in range(n_kv):
            k = kv[:, n * hd:(n + 1) * hd]
            v = kv[:, (n_kv + n) * hd:(n_kv + n + 1) * hd]
            s = _dot_nt(q_ref[sb, n], k) + bias_ref[n]
            o_ref[sb, n] = _sink_softmax_pv(s, sink_ref[n], v).astype(BF16)


def _swa_sample_attn(q4, state, new_pad, bias, sink_rows, *, dec_seq):
    n_seq, n_kv, rows, hd = q4.shape
    n_win, width = state.shape[1], state.shape[2]
    sb = SWA_SEQS_PER_STEP
    assert n_seq % sb == 0
    blk = lambda i: (i, 0, 0)
    return pl.pallas_call(
        functools.partial(_swa_sample_kernel, n_kv=n_kv, hd=hd, dec_seq=dec_seq),
        grid=(n_seq // sb,),
        in_specs=[pl.BlockSpec((sb, n_kv, rows, hd), lambda i: (i, 0, 0, 0)),
                  pl.BlockSpec((sb, n_win, width), blk), pl.BlockSpec((sb,) + new_pad.shape[1:], blk),
                  _resident(bias.shape), _resident(sink_rows.shape)],
        out_specs=[pl.BlockSpec((sb, n_kv, rows, hd), lambda i: (i, 0, 0, 0)), pl.BlockSpec((sb, n_win, width), blk)],
        out_shape=[jax.ShapeDtypeStruct(q4.shape, BF16), jax.ShapeDtypeStruct(state.shape, F32)],
        compiler_params=_params("arbitrary"),
        name="swa_sample_attn",
    )(q4, state, new_pad, bias, sink_rows)


def _proj_residual_kernel(o_ref, w_ref, b_ref, h_ref, out_ref):
    out_ref[...] = h_ref[...] + (_dot(o_ref[...], w_ref[...]) + b_ref[...])


def _proj_residual(o, w, b, h):
    m, d = h.shape
    tm = min(ROW_TILE, m)
    row = lambda i: (i, 0)
    return pl.pallas_call(
        _proj_residual_kernel,
        grid=(m // tm,),
        in_specs=[pl.BlockSpec((tm, o.shape[1]), row), _resident(w.shape), _resident((1, d)), pl.BlockSpec((tm, d), row)],
        out_specs=pl.BlockSpec((tm, d), row),
        out_shape=jax.ShapeDtypeStruct((m, d), F32),
        compiler_params=_params("arbitrary"),
        name="swa_out",
    )(o, w, b, h)


def _rope_tables(pos, rope):
    inv_freq = ROPE_THETA ** (-jnp.arange(0, rope, 2, dtype=F32) / rope)
    ang = pos.astype(F32)[:, None] * inv_freq[None, :]
    cos, sin = jnp.cos(ang), jnp.sin(ang)
    return jnp.concatenate([cos, cos], axis=1), jnp.concatenate([sin, sin], axis=1)


def _rot_half_cols(w):
    half = w.shape[-1] // 2
    return jnp.concatenate([-w[..., half:], w[..., :half]], axis=-1)


def _bucket(dist):
    n = jnp.maximum(dist, 0)
    max_exact = N_BUCKETS // 2
    nf = jnp.maximum(n, 1).astype(F32)
    large = max_exact + (jnp.log(nf / max_exact) / math.log(MAX_DISTANCE / max_exact)
                         * (N_BUCKETS - max_exact)).astype(jnp.int32)
    return jnp.where(n < max_exact, n, jnp.minimum(large, N_BUCKETS - 1))


def _window_bias(rel_bias, dist):
    valid = (dist >= 0) & (dist < WINDOW)
    bias = jnp.moveaxis(jnp.take(rel_bias.astype(F32), _bucket(dist), axis=0), -1, 0)
    return jnp.where(valid[None], bias, NEG_INF)


def _band_bias(rel_bias):
    w = WINDOW
    per_dist = _window_bias(rel_bias, jnp.arange(w, dtype=jnp.int32)[None, :])[:, 0, :]
    heads = per_dist.shape[0]
    neg = lambda n: jnp.full((heads, n), NEG_INF, F32)
    line = jnp.concatenate([neg(w), per_dist[:, ::-1], neg(w)], axis=1)
    skew = jnp.tile(line, (1, w))[:, :w * (3 * w - 1)].reshape(heads, w, 3 * w - 1)
    return skew[:, :, w - 1:3 * w - 1]


def kernel(x_prompt, x_sample, cache_mla, state_kv_win, page_table, attn_norm, ffn_norm, final_norm, mla_wq_a, mla_q_norm, mla_wq_b, mla_wkv_a, mla_kv_norm, mla_wkv_b, mla_wo, kv_norm_shared, w_k_shared, b_k_shared, w_v_shared, b_v_shared, swa_wq, swa_bq, swa_sinks, swa_wo, swa_bo, rel_bias, ffn_w_gate, ffn_w_up, ffn_w_down):
    batch, seq, d = x_prompt.shape
    dec_batch, dec_seq, _ = x_sample.shape
    depth = attn_norm.shape[0]
    n_a = mla_wq_a.shape[0]
    past_len = page_table.shape[1] * PAGE_SIZE
    n_win = state_kv_win.shape[1]
    q_lora = mla_wq_a.shape[2]
    kv_lora = mla_kv_norm.shape[1]
    lat = mla_wkv_a.shape[2]
    rope = lat - kv_lora
    heads_a = mla_wq_b.shape[2]
    nope = mla_wq_b.shape[3] - rope
    heads_b, hd = swa_wq.shape[2], swa_wq.shape[3]
    n_kv = w_k_shared.shape[1]
    group = heads_b // n_kv
    assert seq % WINDOW == 0 and n_win == WINDOW and dec_seq <= 8
    mla_scale = 1.0 / math.sqrt(nope + rope)
    swa_scale = 1.0 / math.sqrt(hd)

    vec = lambda a: a.reshape(1, -1).astype(F32)
    streams = [x_prompt.reshape(batch * seq, d), x_sample.reshape(dec_batch * dec_seq, d)]
    rope_tabs = [_rope_tables(jnp.tile(jnp.arange(seq, dtype=jnp.int32), batch), rope),
                 _rope_tables(jnp.tile(past_len + jnp.arange(dec_seq, dtype=jnp.int32), dec_batch), rope)]

    cache_t = jnp.swapaxes(cache_mla, 2, 3)
    ffn_wg, ffn_wu, ffn_wd = ffn_w_gate.astype(BF16), ffn_w_up.astype(BF16), ffn_w_down.astype(BF16)

    rows_p_all, rows_s_all = [], []
    kv_p = kv_win_sample = None
    for layer in range(depth):
        ga = vec(attn_norm[layer])
        if layer < n_a:
            a = layer
            wkv_a = mla_wkv_a[a]
            w_in = jnp.concatenate([mla_wq_a[a], wkv_a, _rot_half_cols(wkv_a[:, kv_lora:])], axis=1).astype(BF16)
            wq_pe = mla_wq_b[a][:, :, nope:]
            wqb = jnp.concatenate([mla_wq_b[a][:, :, :nope].reshape(q_lora, heads_a * nope),
                                   wq_pe.reshape(q_lora, heads_a * rope),
                                   _rot_half_cols(wq_pe).reshape(q_lora, heads_a * rope)], axis=1).astype(BF16)
            wkt = jnp.transpose(mla_wkv_b[a][:, :, :nope], (1, 2, 0)).astype(BF16)
            wv = jnp.transpose(mla_wkv_b[a][:, :, nope:], (1, 0, 2)).astype(BF16)
            wo = mla_wo[a].reshape(-1, d).astype(BF16)

            stage = []
            for h, (cos2, sin2) in zip(streams, rope_tabs):
                cq, rows, rows_b = _mla_in(h, ga, w_in, vec(mla_q_norm[a]), vec(mla_kv_norm[a]), cos2, sin2,
                                           q_lora=q_lora, kv_lora=kv_lora, rope=rope)
                stage.append((_mla_q(cq, wqb, wkt, cos2, sin2, scale=mla_scale), rows, rows_b))
            (q_p, rows_p, rows_pb), (q_s, rows_s, rows_sb) = stage
            rows_p_all.append(rows_p.reshape(batch, seq, lat))
            rows_s_all.append(rows_s.reshape(dec_batch, dec_seq, lat))

            o_p = _mla_prompt_attn(q_p, rows_pb, batch=batch, seq=seq, kv_lora=kv_lora)
            q_s = q_s.reshape(heads_a, dec_batch, dec_seq, lat).transpose(1, 0, 2, 3).reshape(dec_batch, heads_a * dec_seq, lat)
            new_pad = jnp.pad(rows_sb.reshape(dec_batch, dec_seq, lat), ((0, 0), (0, NEW_ROWS_PAD - dec_seq), (0, 0)))
            o_s = _mla_sample_attn(q_s, new_pad, cache_t, a, page_table, dec_seq=dec_seq, kv_lora=kv_lora)
            o_s = o_s.reshape(dec_batch, heads_a, dec_seq, kv_lora).transpose(1, 0, 2, 3).reshape(heads_a, dec_batch * dec_seq, kv_lora)
            streams = [_mla_out(o, wv, wo, h) for o, h in zip((o_p, o_s), streams)]
        else:
            b = layer - n_a
            wq = swa_wq[b].reshape(d, heads_b * hd).astype(BF16)
            swap_pairs = lambda t: t.reshape(t.shape[0], n_kv // 2, 2, hd)[:, :, ::-1].reshape(t.shape[0], n_kv * hd)
            wk, wv_s = w_k_shared.reshape(d, n_kv * hd), w_v_shared.reshape(d, n_kv * hd)
            bk, bv = b_k_shared.reshape(1, -1), b_v_shared.reshape(1, -1)
            wkv = jnp.concatenate([wk, wv_s, swap_pairs(wk), swap_pairs(wv_s)], axis=1).astype(BF16)
            bkv = jnp.concatenate([bk, bv, swap_pairs(bk), swap_pairs(bv)], axis=1).astype(F32)
            (q_p, kv_p_new, kvb_p), (q_s, kv_s_new, _) = [
                _swa_in(h, ga, vec(kv_norm_shared), wq, vec(swa_bq[b]), wkv, bkv, scale=swa_scale) for h in streams]
            if b == 0:
                kv_p, kv_pb, kv_s = kv_p_new, kvb_p, kv_s_new
                state = state_kv_win.reshape(dec_batch, n_win, 2 * n_kv * hd)
            sinks = swa_sinks[b].astype(F32)

            pairs = group // 2
            bias_p = _band_bias(rel_bias).reshape(n_kv, pairs, 2, WINDOW, 2 * WINDOW)
            bias_p = bias_p.transpose(0, 1, 3, 2, 4).reshape(n_kv, pairs * WINDOW, 4 * WINDOW)
            sink_p = jnp.repeat(sinks.reshape(n_kv, pairs, 2).transpose(0, 2, 1), WINDOW, axis=2)[..., None]
            o_p = _swa_prompt_attn(q_p, kv_pb, bias_p, sink_p, batch=batch, seq=seq, n_kv=n_kv, group=group, hd=hd)

            n_keys = n_win + 8
            t_idx = jnp.arange(dec_seq, dtype=jnp.int32)[:, None]
            c_idx = jnp.arange(n_keys, dtype=jnp.int32)[None, :]
            dist_s = jnp.where(c_idx < n_win + dec_seq, n_win + t_idx - c_idx, -1)
            bias_s = _window_bias(rel_bias, dist_s)
            bias_s = bias_s.reshape(n_kv, group * dec_seq, n_keys)
            sink_s = jnp.repeat(sinks, dec_seq).reshape(n_kv, group * dec_seq, 1)
            q4 = q_s.reshape(dec_batch, dec_seq, n_kv, group, hd).transpose(0, 2, 3, 1, 4).reshape(dec_batch, n_kv, group * dec_seq, hd)
            new_pad = jnp.pad(kv_s.reshape(dec_batch, dec_seq, -1), ((0, 0), (0, 8 - dec_seq), (0, 0)))
            o4, win = _swa_sample_attn(q4, state, new_pad, bias_s, sink_s, dec_seq=dec_seq)
            if b == 0:
                kv_win_sample = win
            o_s = o4.reshape(dec_batch, n_kv, group, dec_seq, hd).transpose(0, 3, 1, 2, 4).reshape(dec_batch * dec_seq, heads_b * hd)
            wo = swa_wo[b].reshape(heads_b * hd, d).astype(BF16)
            streams = [_proj_residual(o, wo, vec(swa_bo[b]), h) for o, h in zip((o_p, o_s), streams)]

        last = layer == depth - 1
        streams = [_ffn(h, vec(ffn_norm[layer]), ffn_wg, ffn_wu, ffn_wd, vec(final_norm), layer, final_norm=last)
                   for h in streams]

    y_prompt = streams[0].reshape(batch, seq, d)
    y_sample = streams[1].reshape(dec_batch, dec_seq, d)
    kv_win_prompt = kv_p.reshape(batch, seq, 2, n_kv, hd)[:, seq - min(WINDOW, seq):]
    kv_win_sample = kv_win_sample.reshape(dec_batch, n_win, 2, n_kv, hd)
    return (y_prompt, y_sample, jnp.stack(rows_p_all, axis=0), jnp.stack(rows_s_all, axis=0),
            kv_win_prompt, kv_win_sample)
```

```python
import functools
import math

import jax
import jax.numpy as jnp
from jax import lax
from jax.experimental import pallas as pl
from jax.experimental.pallas import tpu as pltpu

F32 = jnp.float32
BF16 = jnp.bfloat16

EPS = 1e-6
NEG_INF = -1e30
ROPE_THETA = 10000.0
PAGE_SIZE = 128
WINDOW = 128
N_BUCKETS = 32
MAX_DISTANCE = 128

VMEM_LIMIT_BYTES = 52 * 1024 * 1024
ROW_TILE = 512
FFN_TILE = 512
MLA_Q_TILE = 256
MLA_K_TILE = 512
MLA_HEADS_PER_PASS = 2
PAGES_PER_STEP = 32
SAMPLE_CHAINS = 2
NEW_ROWS_PAD = 16
SWA_SEQS_PER_STEP = 8


def _params(*semantics):
    return pltpu.CompilerParams(dimension_semantics=semantics, vmem_limit_bytes=VMEM_LIMIT_BYTES)


def _rms(x, g):
    return x * lax.rsqrt(jnp.mean(x * x, axis=-1, keepdims=True) + EPS) * g


def _dot(a, b):
    return jnp.dot(a, b, preferred_element_type=F32)


def _dot_nt(a, b):
    return lax.dot_general(a, b, (((1,), (1,)), ((), ())), preferred_element_type=F32)


def _resident(shape):
    return pl.BlockSpec(shape, lambda *_: (0,) * len(shape))


def _mla_in_kernel(h_ref, g_ref, w_ref, qn_ref, kvn_ref, cos_ref, sin_ref,
                   cq_ref, rows_ref, rowsb_ref, *, q_lora, kv_lora, rope):
    xn = _rms(h_ref[...], g_ref[...]).astype(BF16)
    y = _dot(xn, w_ref[...])
    cq_ref[...] = _rms(y[:, :q_lora], qn_ref[...]).astype(BF16)
    c0 = q_lora
    ckv = _rms(y[:, c0:c0 + kv_lora], kvn_ref[...])
    p0 = c0 + kv_lora
    kpe = y[:, p0:p0 + rope] * cos_ref[...] + y[:, p0 + rope:p0 + 2 * rope] * sin_ref[...]
    rows_ref[:, :kv_lora] = ckv
    rows_ref[:, kv_lora:] = kpe
    rowsb_ref[:, :kv_lora] = ckv.astype(BF16)
    rowsb_ref[:, kv_lora:] = kpe.astype(BF16)


def _mla_in(h, g, w, qn, kvn, cos2, sin2, *, q_lora, kv_lora, rope):
    m, d = h.shape
    tm = min(ROW_TILE, m)
    lat = kv_lora + rope
    row = lambda i: (i, 0)
    return pl.pallas_call(
        functools.partial(_mla_in_kernel, q_lora=q_lora, kv_lora=kv_lora, rope=rope),
        grid=(m // tm,),
        in_specs=[pl.BlockSpec((tm, d), row), _resident((1, d)), _resident(w.shape),
                  _resident((1, q_lora)), _resident((1, kv_lora)),
                  pl.BlockSpec((tm, rope), row), pl.BlockSpec((tm, rope), row)],
        out_specs=[pl.BlockSpec((tm, q_lora), row), pl.BlockSpec((tm, lat), row), pl.BlockSpec((tm, lat), row)],
        out_shape=[jax.ShapeDtypeStruct((m, q_lora), BF16), jax.ShapeDtypeStruct((m, lat), F32),
                   jax.ShapeDtypeStruct((m, lat), BF16)],
        compiler_params=_params("arbitrary"),
        name="mla_in",
    )(h, g, w, qn, kvn, cos2, sin2)


def _mla_q_kernel(cq_ref, wqb_ref, wkt_ref, cos_ref, sin_ref, q_ref, *, heads, nope, rope, kv_lora, scale):
    y = _dot(cq_ref[...], wqb_ref[...])
    cos = cos_ref[...]
    sin = sin_ref[...]
    pe0 = heads * nope
    rot0 = pe0 + heads * rope
    for h in range(heads):
        qn = y[:, h * nope:(h + 1) * nope].astype(BF16)
        q_ref[h, :, :kv_lora] = (_dot(qn, wkt_ref[h]) * scale).astype(BF16)
        pe = y[:, pe0 + h * rope:pe0 + (h + 1) * rope] * cos + y[:, rot0 + h * rope:rot0 + (h + 1) * rope] * sin
        q_ref[h, :, kv_lora:] = (pe * scale).astype(BF16)


def _mla_q(cq, wqb, wkt, cos2, sin2, *, scale):
    m, q_lora = cq.shape
    heads, nope, kv_lora = wkt.shape
    rope = cos2.shape[1]
    tm = min(256, m)
    row = lambda i: (i, 0)
    return pl.pallas_call(
        functools.partial(_mla_q_kernel, heads=heads, nope=nope, rope=rope, kv_lora=kv_lora, scale=scale),
        grid=(m // tm,),
        in_specs=[pl.BlockSpec((tm, q_lora), row), _resident(wqb.shape), _resident(wkt.shape),
                  pl.BlockSpec((tm, rope), row), pl.BlockSpec((tm, rope), row)],
        out_specs=pl.BlockSpec((heads, tm, kv_lora + rope), lambda i: (0, i, 0)),
        out_shape=jax.ShapeDtypeStruct((heads, m, kv_lora + rope), BF16),
        compiler_params=_params("arbitrary"),
        name="mla_q",
    )(cq, wqb, wkt, cos2, sin2)


def _softmax_step(sc, pv, m_sc, l_sc, acc_sc, r=slice(None)):
    m_prev = m_sc[r]
    m_new = jnp.maximum(m_prev, jnp.max(sc, axis=1, keepdims=True))
    alpha = jnp.exp(m_prev - m_new)
    p = jnp.exp(sc - m_new)
    l_sc[r] = alpha * l_sc[r] + jnp.sum(p, axis=1, keepdims=True)
    acc_sc[r] = alpha * acc_sc[r] + pv(p.astype(BF16))
    m_sc[r] = m_new


def _mla_prompt_kernel(qi_ref, kj_ref, q_ref, k_ref, o_ref, m_sc, l_sc, acc_sc, *, heads, tq, tk, kv_lora):
    s = pl.program_id(1)
    qi = qi_ref[s]
    kj = kj_ref[s]
    diag = (qi * tq + tq - 1) // tk
    hp = MLA_HEADS_PER_PASS
    rows = hp * tq

    @pl.when(kj == 0)
    def _():
        m_sc[...] = jnp.full_like(m_sc, NEG_INF)
        l_sc[...] = jnp.zeros_like(l_sc)
        acc_sc[...] = jnp.zeros_like(acc_sc)

    def step(mask):
        k = k_ref[...]
        v = k[:, :kv_lora]

        def scores(g):
            sc = _dot_nt(q_ref[g * hp:(g + 1) * hp].reshape(rows, q_ref.shape[-1]), k)
            return sc if mask is None else jnp.where(mask, sc, NEG_INF)

        n_groups = heads // hp
        sc = scores(0)
        for g in range(n_groups):
            nxt = scores(g + 1) if g + 1 < n_groups else None
            _softmax_step(sc, lambda p: _dot(p, v), m_sc, l_sc, acc_sc, slice(g * rows, (g + 1) * rows))
            sc = nxt

    @pl.when(kj != diag)
    def _():
        step(None)

    @pl.when(kj == diag)
    def _():
        q_pos = qi * tq + (lax.broadcasted_iota(jnp.int32, (rows, tk), 0) & (tq - 1))
        k_pos = kj * tk + lax.broadcasted_iota(jnp.int32, (rows, tk), 1)
        step(k_pos <= q_pos)
        o_ref[...] = (acc_sc[...] / l_sc[...]).astype(BF16).reshape(o_ref.shape)


def _mla_prompt_attn(q_full, rows_b, *, batch, seq, kv_lora):
    heads, _, lat = q_full.shape
    tq, tk = MLA_Q_TILE, MLA_K_TILE
    assert tq & (tq - 1) == 0 and seq % tq == 0 and seq % tk == 0
    nq, nk = seq // tq, seq // tk
    qi_list, kj_list = [], []
    for qi in range(nq):
        for kj in range((qi * tq + tq - 1) // tk + 1):
            qi_list.append(qi)
            kj_list.append(kj)
    qi_tab = jnp.asarray(qi_list, jnp.int32)
    kj_tab = jnp.asarray(kj_list, jnp.int32)
    rows = heads * tq
    grid_spec = pltpu.PrefetchScalarGridSpec(
        num_scalar_prefetch=2,
        grid=(batch, len(qi_list)),
        in_specs=[pl.BlockSpec((heads, tq, lat), lambda b, s, qi, kj: (0, b * nq + qi[s], 0)),
                  pl.BlockSpec((tk, lat), lambda b, s, qi, kj: (b * nk + kj[s], 0))],
        out_specs=pl.BlockSpec((heads, tq, kv_lora), lambda b, s, qi, kj: (0, b * nq + qi[s], 0)),
        scratch_shapes=[pltpu.VMEM((rows, 1), F32), pltpu.VMEM((rows, 1), F32), pltpu.VMEM((rows, kv_lora), F32)],
    )
    return pl.pallas_call(
        functools.partial(_mla_prompt_kernel, heads=heads, tq=tq, tk=tk, kv_lora=kv_lora),
        grid_spec=grid_spec,
        out_shape=jax.ShapeDtypeStruct((heads, batch * seq, kv_lora), BF16),
        compiler_params=_params("arbitrary", "arbitrary"),
        name="mla_prompt_attn",
    )(qi_tab, kj_tab, q_full, rows_b)


def _mla_sample_kernel(pt_ref, q_ref, new_ref, *refs, n_pages, dec_seq, kv_lora):
    page_refs = refs[:n_pages]
    o_ref, m_sc, l_sc, acc_sc = refs[n_pages:]
    g = pl.program_id(1)

    @pl.when(g == 0)
    def _():
        m_sc[...] = jnp.full_like(m_sc, NEG_INF)
        l_sc[...] = jnp.zeros_like(l_sc)
        acc_sc[...] = jnp.zeros_like(acc_sc)

    q = q_ref[...]
    chains = m_sc.shape[0]
    per = n_pages // chains
    kts, scs = [], []
    for c in range(chains):
        kts.append(jnp.concatenate([r[...].astype(BF16) for r in page_refs[c * per:(c + 1) * per]], axis=1))
        scs.append(_dot(q, kts[c]))
    for c in range(chains):
        vt = kts[c][:kv_lora]
        _softmax_step(scs[c], lambda p, vt=vt: _dot_nt(p, vt), m_sc, l_sc, acc_sc, c)

    @pl.when(g == pl.num_programs(1) - 1)
    def _():
        kn = new_ref[...]
        sc = _dot_nt(q, kn)
        t = lax.broadcasted_iota(jnp.int32, sc.shape, 0) % dec_seq
        c = lax.broadcasted_iota(jnp.int32, sc.shape, 1)
        _softmax_step(jnp.where(c <= t, sc, NEG_INF), lambda p: _dot(p, kn[:, :kv_lora]), m_sc, l_sc, acc_sc, 0)
        m = functools.reduce(jnp.maximum, [m_sc[c] for c in range(chains)])
        weights = [jnp.exp(m_sc[c] - m) for c in range(chains)]
        l = sum(l_sc[c] * weights[c] for c in range(chains))
        acc = sum(acc_sc[c] * weights[c] for c in range(chains))
        o_ref[...] = (acc / l).astype(BF16)


def _mla_sample_attn(q_s, new_pad, cache_t, layer, page_table, *, dec_seq, kv_lora):
    n_seq, rows, lat = q_s.shape
    n_pages_total = page_table.shape[1]
    p = PAGES_PER_STEP
    assert n_pages_total % p == 0 and p % SAMPLE_CHAINS == 0 and cache_t.shape[2:] == (lat, PAGE_SIZE)
    page_specs = [
        pl.BlockSpec((None, None, lat, PAGE_SIZE), lambda s, g, pt, j=j: (layer, pt[s, g * p + j], 0, 0))
        for j in range(p)
    ]
    grid_spec = pltpu.PrefetchScalarGridSpec(
        num_scalar_prefetch=1,
        grid=(n_seq, n_pages_total // p),
        in_specs=[pl.BlockSpec((None, rows, lat), lambda s, g, pt: (s, 0, 0)),
                  pl.BlockSpec((None, NEW_ROWS_PAD, lat), lambda s, g, pt: (s, 0, 0))] + page_specs,
        out_specs=pl.BlockSpec((None, rows, kv_lora), lambda s, g, pt: (s, 0, 0)),
        scratch_shapes=[pltpu.VMEM((SAMPLE_CHAINS, rows, 1), F32), pltpu.VMEM((SAMPLE_CHAINS, rows, 1), F32),
                        pltpu.VMEM((SAMPLE_CHAINS, rows, kv_lora), F32)],
    )
    return pl.pallas_call(
        functools.partial(_mla_sample_kernel, n_pages=p, dec_seq=dec_seq, kv_lora=kv_lora),
        grid_spec=grid_spec,
        out_shape=jax.ShapeDtypeStruct((n_seq, rows, kv_lora), BF16),
        compiler_params=_params("arbitrary", "arbitrary"),
        name="mla_sample_attn",
    )(page_table, q_s, new_pad, *([cache_t] * p))


def _mla_out_kernel(o_ref, wv_ref, wo_ref, h_ref, out_ref, *, heads):
    v = jnp.concatenate([_dot(o_ref[h], wv_ref[h]).astype(BF16) for h in range(heads)], axis=1)
    out_ref[...] = h_ref[...] + _dot(v, wo_ref[...])


def _mla_out(o_lat, wv, wo, h):
    heads, m, kv_lora = o_lat.shape
    d = h.shape[1]
    tm = min(256, m)
    row = lambda i: (i, 0)
    return pl.pallas_call(
        functools.partial(_mla_out_kernel, heads=heads),
        grid=(m // tm,),
        in_specs=[pl.BlockSpec((heads, tm, kv_lora), lambda i: (0, i, 0)), _resident(wv.shape), _resident(wo.shape),
                  pl.BlockSpec((tm, d), row)],
        out_specs=pl.BlockSpec((tm, d), row),
        out_shape=jax.ShapeDtypeStruct((m, d), F32),
        compiler_params=_params("arbitrary"),
        name="mla_out",
    )(o_lat, wv, wo, h)


def _ffn_kernel(h_ref, g_ref, wg_ref, wu_ref, wd_ref, fg_ref, out_ref, xn_sc, acc_sc, *, final_norm):
    j = pl.program_id(1)

    @pl.when(j == 0)
    def _():
        xn_sc[...] = _rms(h_ref[...], g_ref[...]).astype(BF16)
        acc_sc[...] = jnp.zeros_like(acc_sc)

    xn = xn_sc[...]
    a = _dot(xn, wg_ref[...])
    u = _dot(xn, wu_ref[...])
    acc_sc[...] += _dot((a * jax.nn.sigmoid(a) * u).astype(BF16), wd_ref[...])

    @pl.when(j == pl.num_programs(1) - 1)
    def _():
        res = h_ref[...] + acc_sc[...]
        out_ref[...] = _rms(res, fg_ref[...]) if final_norm else res


def _ffn(h, g, wg, wu, wd, fg, layer, *, final_norm):
    m, d = h.shape
    dff = wg.shape[2]
    tm = min(ROW_TILE, m)
    tf = FFN_TILE
    assert dff % tf == 0
    row = lambda i, j: (i, 0)
    return pl.pallas_call(
        functools.partial(_ffn_kernel, final_norm=final_norm),
        grid=(m // tm, dff // tf),
        in_specs=[pl.BlockSpec((tm, d), row), _resident((1, d)),
                  pl.BlockSpec((None, d, tf), lambda i, j: (layer, 0, j)),
                  pl.BlockSpec((None, d, tf), lambda i, j: (layer, 0, j)),
                  pl.BlockSpec((None, tf, d), lambda i, j: (layer, j, 0)), _resident((1, d))],
        out_specs=pl.BlockSpec((tm, d), row),
        out_shape=jax.ShapeDtypeStruct((m, d), F32),
        scratch_shapes=[pltpu.VMEM((tm, d), BF16), pltpu.VMEM((tm, d), F32)],
        compiler_params=_params("arbitrary", "arbitrary"),
        name="ffn",
    )(h, g, wg, wu, wd, fg)


def _swa_in_kernel(h_ref, ga_ref, gk_ref, wq_ref, bq_ref, wkv_ref, bkv_ref, q_ref, kv_ref, kvb_ref, *, scale):
    x = h_ref[...]
    y = x * lax.rsqrt(jnp.mean(x * x, axis=-1, keepdims=True) + EPS)
    xn = (y * ga_ref[...]).astype(BF16)
    hn = (y * gk_ref[...]).astype(BF16)
    q_ref[...] = ((_dot(xn, wq_ref[...]) + bq_ref[...]) * scale).astype(BF16)
    kv = _dot(hn, wkv_ref[...]) + bkv_ref[...]
    kv_ref[...] = kv[:, :kv_ref.shape[1]]
    kvb_ref[...] = kv.astype(BF16)


def _swa_in(h, ga, gk, wq, bq, wkv, bkv, *, scale):
    m, d = h.shape
    nq, nkv = wq.shape[1], wkv.shape[1]
    tm = min(256, m)
    row = lambda i: (i, 0)
    return pl.pallas_call(
        functools.partial(_swa_in_kernel, scale=scale),
        grid=(m // tm,),
        in_specs=[pl.BlockSpec((tm, d), row), _resident((1, d)), _resident((1, d)), _resident(wq.shape),
                  _resident((1, nq)), _resident(wkv.shape), _resident((1, nkv))],
        out_specs=[pl.BlockSpec((tm, nq), row), pl.BlockSpec((tm, nkv // 2), row), pl.BlockSpec((tm, nkv), row)],
        out_shape=[jax.ShapeDtypeStruct((m, nq), BF16), jax.ShapeDtypeStruct((m, nkv // 2), F32),
                   jax.ShapeDtypeStruct((m, nkv), BF16)],
        compiler_params=_params("arbitrary"),
        name="swa_in",
    )(h, ga, gk, wq, bq, wkv, bkv)


def _sink_softmax_pv(s, sink, v):
    m = jnp.maximum(jnp.max(s, axis=1, keepdims=True), sink)
    p = jnp.exp(s - m)
    denom = jnp.sum(p, axis=1, keepdims=True) + jnp.exp(sink - m)
    return _dot(p.astype(BF16), v) / denom


def _swa_prompt_kernel(q_ref, prev_ref, cur_ref, bias_ref, sink_ref, o_ref, *, n_kv, group, hd):
    i = pl.program_id(1)
    w = q_ref.shape[0]
    lanes = 2 * hd
    pairs = group // 2
    band = 2 * w
    q = q_ref[...]
    kv = jnp.concatenate([prev_ref[...], cur_ref[...]], axis=0)
    col = lax.broadcasted_iota(jnp.int32, (1, 2 * band), 1)
    in_seq = jnp.logical_or((col & (band - 1)) >= w, i > 0)
    low = lax.broadcasted_iota(jnp.int32, (1, lanes), 1) < hd
    zero = jnp.zeros((), BF16)
    swapped = n_kv * lanes

    def block_diag(slab0, n):
        nat = kv[:, slab0 + (n // 2) * lanes:slab0 + (n // 2 + 1) * lanes]
        swp = kv[:, swapped + slab0 + (n // 2) * lanes:swapped + slab0 + (n // 2 + 1) * lanes]
        first, second = (nat, swp) if n % 2 == 0 else (swp, nat)
        return jnp.concatenate([jnp.where(low, first, zero), jnp.where(low, zero, second)], axis=0)

    for n in range(n_kv):
        k2 = block_diag(0, n)
        v2 = block_diag(n_kv * hd, n)
        qn = jnp.concatenate([q[:, (n * pairs + p) * lanes:(n * pairs + p + 1) * lanes] for p in range(pairs)], axis=0)
        s = jnp.where(in_seq, _dot_nt(qn, k2) + bias_ref[n], NEG_INF)
        probs, recips = [], []
        for half in range(2):
            sh = s[:, half * band:(half + 1) * band]
            sink = sink_ref[n, half]
            m = jnp.maximum(jnp.max(sh, axis=1, keepdims=True), sink)
            p = jnp.exp(sh - m)
            probs.append(p.astype(BF16))
            recips.append(1.0 / (jnp.sum(p, axis=1, keepdims=True) + jnp.exp(sink - m)))
        o = _dot(jnp.concatenate(probs, axis=1), v2) * jnp.where(low, recips[0], recips[1])
        for p in range(pairs):
            o_ref[:, (n * pairs + p) * lanes:(n * pairs + p + 1) * lanes] = o[p * w:(p + 1) * w].astype(BF16)


def _swa_prompt_attn(q, kvb, bias, sink_rows, *, batch, seq, n_kv, group, hd):
    w = WINDOW
    nb = seq // w
    assert group % 2 == 0 and kvb.shape[1] == 4 * n_kv * hd
    return pl.pallas_call(
        functools.partial(_swa_prompt_kernel, n_kv=n_kv, group=group, hd=hd),
        grid=(batch, nb),
        in_specs=[pl.BlockSpec((w, q.shape[1]), lambda b, i: (b * nb + i, 0)),
                  pl.BlockSpec((w, kvb.shape[1]), lambda b, i: (b * nb + jnp.maximum(i - 1, 0), 0)),
                  pl.BlockSpec((w, kvb.shape[1]), lambda b, i: (b * nb + i, 0)),
                  _resident(bias.shape), _resident(sink_rows.shape)],
        out_specs=pl.BlockSpec((w, q.shape[1]), lambda b, i: (b * nb + i, 0)),
        out_shape=jax.ShapeDtypeStruct(q.shape, BF16),
        compiler_params=_params("arbitrary", "arbitrary"),
        name="swa_prompt_attn",
    )(q, kvb, kvb, bias, sink_rows)


def _swa_sample_kernel(q_ref, state_ref, new_ref, bias_ref, sink_ref, o_ref, win_ref, *, n_kv, hd, dec_seq):
    n_win = state_ref.shape[2]
    lane = lax.broadcasted_iota(jnp.int32, (1, n_win), 1)
    for sb in range(q_ref.shape[0]):
        state_t = state_ref[sb]
        new = new_ref[sb]
        lead = jnp.zeros((n_win - new.shape[0], new.shape[1]), F32)
        new_t = jnp.concatenate([lead, new], axis=0).T
        win_ref[sb] = jnp.where(lane < n_win - dec_seq, pltpu.roll(state_t, n_win - dec_seq, axis=1), new_t)
        kv_t = jnp.concatenate([state_t, new_t], axis=1).astype(BF16)
        for n in range(n_kv):
            kt = kv_t[n * hd:(n + 1) * hd]
            vt = kv_t[(n_kv + n) * hd:(n_kv + n + 1) * hd]
            s = _dot(q_ref[sb, n], kt) + bias_ref[n]
            sink = sink_ref[n]
            m = jnp.maximum(jnp.max(s, axis=1, keepdims=True), sink)
            p = jnp.exp(s - m)
            denom = jnp.sum(p, axis=1, keepdims=True) + jnp.exp(sink - m)
            o_ref[sb, n] = (_dot_nt(p.astype(BF16), vt) / denom).astype(BF16)


def _swa_sample_attn(q4, state_t, new_pad, bias, sink_rows, *, dec_seq):
    n_seq, n_kv, rows, hd = q4.shape
    width, n_win = state_t.shape[1], state_t.shape[2]
    sb = SWA_SEQS_PER_STEP
    assert n_seq % sb == 0 and bias.shape[2] == 2 * n_win
    blk = lambda i: (i, 0, 0)
    return pl.pallas_call(
        functools.partial(_swa_sample_kernel, n_kv=n_kv, hd=hd, dec_seq=dec_seq),
        grid=(n_seq // sb,),
        in_specs=[pl.BlockSpec((sb, n_kv, rows, hd), lambda i: (i, 0, 0, 0)),
                  pl.BlockSpec((sb, width, n_win), blk), pl.BlockSpec((sb,) + new_pad.shape[1:], blk),
                  _resident(bias.shape), _resident(sink_rows.shape)],
        out_specs=[pl.BlockSpec((sb, n_kv, rows, hd), lambda i: (i, 0, 0, 0)), pl.BlockSpec((sb, width, n_win), blk)],
        out_shape=[jax.ShapeDtypeStruct(q4.shape, BF16), jax.ShapeDtypeStruct(state_t.shape, F32)],
        compiler_params=_params("arbitrary"),
        name="swa_sample_attn",
    )(q4, state_t, new_pad, bias, sink_rows)


def _proj_residual_kernel(o_ref, w_ref, b_ref, h_ref, out_ref):
    out_ref[...] = h_ref[...] + (_dot(o_ref[...], w_ref[...]) + b_ref[...])


def _proj_residual(o, w, b, h):
    m, d = h.shape
    tm = min(ROW_TILE, m)
    row = lambda i: (i, 0)
    return pl.pallas_call(
        _proj_residual_kernel,
        grid=(m // tm,),
        in_specs=[pl.BlockSpec((tm, o.shape[1]), row), _resident(w.shape), _resident((1, d)), pl.BlockSpec((tm, d), row)],
        out_specs=pl.BlockSpec((tm, d), row),
        out_shape=jax.ShapeDtypeStruct((m, d), F32),
        compiler_params=_params("arbitrary"),
        name="swa_out",
    )(o, w, b, h)


def _rope_tables(pos, rope):
    inv_freq = ROPE_THETA ** (-jnp.arange(0, rope, 2, dtype=F32) / rope)
    ang = pos.astype(F32)[:, None] * inv_freq[None, :]
    cos, sin = jnp.cos(ang), jnp.sin(ang)
    return jnp.concatenate([cos, cos], axis=1), jnp.concatenate([sin, sin], axis=1)


def _rot_half_cols(w):
    half = w.shape[-1] // 2
    return jnp.concatenate([-w[..., half:], w[..., :half]], axis=-1)


def _bucket(dist):
    n = jnp.maximum(dist, 0)
    max_exact = N_BUCKETS // 2
    nf = jnp.maximum(n, 1).astype(F32)
    large = max_exact + (jnp.log(nf / max_exact) / math.log(MAX_DISTANCE / max_exact)
                         * (N_BUCKETS - max_exact)).astype(jnp.int32)
    return jnp.where(n < max_exact, n, jnp.minimum(large, N_BUCKETS - 1))


def _window_bias(rel_bias, dist):
    valid = (dist >= 0) & (dist < WINDOW)
    bias = jnp.moveaxis(jnp.take(rel_bias.astype(F32), _bucket(dist), axis=0), -1, 0)
    return jnp.where(valid[None], bias, NEG_INF)


def _band_bias(rel_bias):
    w = WINDOW
    per_dist = _window_bias(rel_bias, jnp.arange(w, dtype=jnp.int32)[None, :])[:, 0, :]
    heads = per_dist.shape[0]
    neg = lambda n: jnp.full((heads, n), NEG_INF, F32)
    line = jnp.concatenate([neg(w), per_dist[:, ::-1], neg(w)], axis=1)
    skew = jnp.tile(line, (1, w))[:, :w * (3 * w - 1)].reshape(heads, w, 3 * w - 1)
    return skew[:, :, w - 1:3 * w - 1]


def kernel(x_prompt, x_sample, cache_mla, state_kv_win, page_table, attn_norm, ffn_norm, final_norm, mla_wq_a, mla_q_norm, mla_wq_b, mla_wkv_a, mla_kv_norm, mla_wkv_b, mla_wo, kv_norm_shared, w_k_shared, b_k_shared, w_v_shared, b_v_shared, swa_wq, swa_bq, swa_sinks, swa_wo, swa_bo, rel_bias, ffn_w_gate, ffn_w_up, ffn_w_down):
    batch, seq, d = x_prompt.shape
    dec_batch, dec_seq, _ = x_sample.shape
    depth = attn_norm.shape[0]
    n_a = mla_wq_a.shape[0]
    past_len = page_table.shape[1] * PAGE_SIZE
    n_win = state_kv_win.shape[1]
    q_lora = mla_wq_a.shape[2]
    kv_lora = mla_kv_norm.shape[1]
    lat = mla_wkv_a.shape[2]
    rope = lat - kv_lora
    heads_a = mla_wq_b.shape[2]
    nope = mla_wq_b.shape[3] - rope
    heads_b, hd = swa_wq.shape[2], swa_wq.shape[3]
    n_kv = w_k_shared.shape[1]
    group = heads_b // n_kv
    assert seq % WINDOW == 0 and n_win == WINDOW and dec_seq <= 8
    mla_scale = 1.0 / math.sqrt(nope + rope)
    swa_scale = 1.0 / math.sqrt(hd)

    vec = lambda a: a.reshape(1, -1).astype(F32)
    streams = [x_prompt.reshape(batch * seq, d), x_sample.reshape(dec_batch * dec_seq, d)]
    rope_tabs = [_rope_tables(jnp.tile(jnp.arange(seq, dtype=jnp.int32), batch), rope),
                 _rope_tables(jnp.tile(past_len + jnp.arange(dec_seq, dtype=jnp.int32), dec_batch), rope)]

    cache_t = jnp.swapaxes(cache_mla, 2, 3)
    ffn_wg, ffn_wu, ffn_wd = ffn_w_gate.astype(BF16), ffn_w_up.astype(BF16), ffn_w_down.astype(BF16)

    rows_p_all, rows_s_all = [], []
    kv_p = kv_win_sample = None
    for layer in range(depth):
        ga = vec(attn_norm[layer])
        if layer < n_a:
            a = layer
            wkv_a = mla_wkv_a[a]
            w_in = jnp.concatenate([mla_wq_a[a], wkv_a, _rot_half_cols(wkv_a[:, kv_lora:])], axis=1).astype(BF16)
            wq_pe = mla_wq_b[a][:, :, nope:]
            wqb = jnp.concatenate([mla_wq_b[a][:, :, :nope].reshape(q_lora, heads_a * nope),
                                   wq_pe.reshape(q_lora, heads_a * rope),
                                   _rot_half_cols(wq_pe).reshape(q_lora, heads_a * rope)], axis=1).astype(BF16)
            wkt = jnp.transpose(mla_wkv_b[a][:, :, :nope], (1, 2, 0)).astype(BF16)
            wv = jnp.transpose(mla_wkv_b[a][:, :, nope:], (1, 0, 2)).astype(BF16)
            wo = mla_wo[a].reshape(-1, d).astype(BF16)

            stage = []
            for h, (cos2, sin2) in zip(streams, rope_tabs):
                cq, rows, rows_b = _mla_in(h, ga, w_in, vec(mla_q_norm[a]), vec(mla_kv_norm[a]), cos2, sin2,
                                           q_lora=q_lora, kv_lora=kv_lora, rope=rope)
                stage.append((_mla_q(cq, wqb, wkt, cos2, sin2, scale=mla_scale), rows, rows_b))
            (q_p, rows_p, rows_pb), (q_s, rows_s, rows_sb) = stage
            rows_p_all.append(rows_p.reshape(batch, seq, lat))
            rows_s_all.append(rows_s.reshape(dec_batch, dec_seq, lat))

            o_p = _mla_prompt_attn(q_p, rows_pb, batch=batch, seq=seq, kv_lora=kv_lora)
            q_s = q_s.reshape(heads_a, dec_batch, dec_seq, lat).transpose(1, 0, 2, 3).reshape(dec_batch, heads_a * dec_seq, lat)
            new_pad = jnp.pad(rows_sb.reshape(dec_batch, dec_seq, lat), ((0, 0), (0, NEW_ROWS_PAD - dec_seq), (0, 0)))
            o_s = _mla_sample_attn(q_s, new_pad, cache_t, a, page_table, dec_seq=dec_seq, kv_lora=kv_lora)
            o_s = o_s.reshape(dec_batch, heads_a, dec_seq, kv_lora).transpose(1, 0, 2, 3).reshape(heads_a, dec_batch * dec_seq, kv_lora)
            streams = [_mla_out(o, wv, wo, h) for o, h in zip((o_p, o_s), streams)]
        else:
            b = layer - n_a
            wq = swa_wq[b].reshape(d, heads_b * hd).astype(BF16)
            swap_pairs = lambda t: t.reshape(t.shape[0], n_kv // 2, 2, hd)[:, :, ::-1].reshape(t.shape[0], n_kv * hd)
            wk, wv_s = w_k_shared.reshape(d, n_kv * hd), w_v_shared.reshape(d, n_kv * hd)
            bk, bv = b_k_shared.reshape(1, -1), b_v_shared.reshape(1, -1)
            wkv = jnp.concatenate([wk, wv_s, swap_pairs(wk), swap_pairs(wv_s)], axis=1).astype(BF16)
            bkv = jnp.concatenate([bk, bv, swap_pairs(bk), swap_pairs(bv)], axis=1).astype(F32)
            (q_p, kv_p_new, kvb_p), (q_s, kv_s_new, _) = [
                _swa_in(h, ga, vec(kv_norm_shared), wq, vec(swa_bq[b]), wkv, bkv, scale=swa_scale) for h in streams]
            if b == 0:
                kv_p, kv_pb, kv_s = kv_p_new, kvb_p, kv_s_new
                state = jnp.transpose(state_kv_win, (0, 2, 3, 4, 1)).reshape(dec_batch, 2 * n_kv * hd, n_win)
            sinks = swa_sinks[b].astype(F32)

            pairs = group // 2
            bias_p = _band_bias(rel_bias).reshape(n_kv, pairs, 2, WINDOW, 2 * WINDOW)
            bias_p = bias_p.transpose(0, 1, 3, 2, 4).reshape(n_kv, pairs * WINDOW, 4 * WINDOW)
            sink_p = jnp.repeat(sinks.reshape(n_kv, pairs, 2).transpose(0, 2, 1), WINDOW, axis=2)[..., None]
            o_p = _swa_prompt_attn(q_p, kv_pb, bias_p, sink_p, batch=batch, seq=seq, n_kv=n_kv, group=group, hd=hd)

            n_keys = 2 * n_win
            t_idx = jnp.arange(dec_seq, dtype=jnp.int32)[:, None]
            j_idx = jnp.arange(n_keys, dtype=jnp.int32)[None, :]
            c_idx = jnp.where(j_idx < n_win, j_idx, j_idx - (n_win - dec_seq))
            real = (j_idx < n_win) | (j_idx >= n_keys - dec_seq)
            dist_s = jnp.where(real, n_win + t_idx - c_idx, -1)
            bias_s = _window_bias(rel_bias, dist_s)
            bias_s = bias_s.reshape(n_kv, group * dec_seq, n_keys)
            sink_s = jnp.repeat(sinks, dec_seq).reshape(n_kv, group * dec_seq, 1)
            q4 = q_s.reshape(dec_batch, dec_seq, n_kv, group, hd).transpose(0, 2, 3, 1, 4).reshape(dec_batch, n_kv, group * dec_seq, hd)
            new_pad = jnp.pad(kv_s.reshape(dec_batch, dec_seq, -1), ((0, 0), (8 - dec_seq, 0), (0, 0)))
            o4, win = _swa_sample_attn(q4, state, new_pad, bias_s, sink_s, dec_seq=dec_seq)
            if b == 0:
                kv_win_sample = win
            o_s = o4.reshape(dec_batch, n_kv, group, dec_seq, hd).transpose(0, 3, 1, 2, 4).reshape(dec_batch * dec_seq, heads_b * hd)
            wo = swa_wo[b].reshape(heads_b * hd, d).astype(BF16)
            streams = [_proj_residual(o, wo, vec(swa_bo[b]), h) for o, h in zip((o_p, o_s), streams)]

        last = layer == depth - 1
        streams = [_ffn(h, vec(ffn_norm[layer]), ffn_wg, ffn_wu, ffn_wd, vec(final_norm), layer, final_norm=last)
                   for h in streams]

    y_prompt = streams[0].reshape(batch, seq, d)
    y_sample = streams[1].reshape(dec_batch, dec_seq, d)
    n_tail = min(WINDOW, seq)
    kv_win_prompt = kv_p.reshape(batch, seq, -1)[:, seq - n_tail:].reshape(batch, n_tail, 2, n_kv, hd)
    kv_win_sample = jnp.transpose(kv_win_sample.reshape(dec_batch, 2, n_kv, hd, n_win), (0, 4, 1, 2, 3))
    return (y_prompt, y_sample, jnp.stack(rows_p_all, axis=0), jnp.stack(rows_s_all, axis=0),
            kv_win_prompt, kv_win_sample)
```

```python
import functools
import math

import jax
import jax.numpy as jnp
from jax import lax
from jax.experimental import pallas as pl
from jax.experimental.pallas import tpu as pltpu

F32 = jnp.float32
BF16 = jnp.bfloat16

EPS = 1e-6
NEG_INF = -1e30
ROPE_THETA = 10000.0
PAGE_SIZE = 128
WINDOW = 128
N_BUCKETS = 32
MAX_DISTANCE = 128

VMEM_LIMIT_BYTES = 52 * 1024 * 1024
ROW_TILE = 512
FFN_TILE = 512
FFN_CHUNKS = 2
MLA_Q_TILE = 256
MLA_K_TILE = 512
MLA_HEADS_PER_PASS = 2
PAGES_PER_STEP = 32
SAMPLE_CHAINS = 2
NEW_ROWS_PAD = 16
SWA_SEQS_PER_STEP = 8


def _params(*semantics):
    return pltpu.CompilerParams(dimension_semantics=semantics, vmem_limit_bytes=VMEM_LIMIT_BYTES)


def _rms(x, g):
    return x * lax.rsqrt(jnp.mean(x * x, axis=-1, keepdims=True) + EPS) * g


def _dot(a, b):
    return jnp.dot(a, b, preferred_element_type=F32)


def _dot_nt(a, b):
    return lax.dot_general(a, b, (((1,), (1,)), ((), ())), preferred_element_type=F32)


def _resident(shape):
    return pl.BlockSpec(shape, lambda *_: (0,) * len(shape))


def _table_index(table_rows, tm):
    assert table_rows % tm == 0
    blocks = table_rows // tm
    return lambda i: (i % blocks, 0)


def _mla_in_kernel(h_ref, g_ref, w_ref, qn_ref, kvn_ref, cos_ref, sin_ref,
                   cq_ref, rows_ref, rowsb_ref, *, q_lora, kv_lora, rope):
    xn = _rms(h_ref[...], g_ref[...]).astype(BF16)
    y = _dot(xn, w_ref[...])
    cq_ref[...] = _rms(y[:, :q_lora], qn_ref[...]).astype(BF16)
    c0 = q_lora
    ckv = _rms(y[:, c0:c0 + kv_lora], kvn_ref[...])
    p0 = c0 + kv_lora
    kpe = y[:, p0:p0 + rope] * cos_ref[...] + y[:, p0 + rope:p0 + 2 * rope] * sin_ref[...]
    rows_ref[:, :kv_lora] = ckv
    rows_ref[:, kv_lora:] = kpe
    rowsb_ref[:, :kv_lora] = ckv.astype(BF16)
    rowsb_ref[:, kv_lora:] = kpe.astype(BF16)


def _mla_in(h, g, w, qn, kvn, cos2, sin2, *, q_lora, kv_lora, rope):
    m, d = h.shape
    tm = min(ROW_TILE, m)
    lat = kv_lora + rope
    row = lambda i: (i, 0)
    tab = _table_index(cos2.shape[0], tm)
    return pl.pallas_call(
        functools.partial(_mla_in_kernel, q_lora=q_lora, kv_lora=kv_lora, rope=rope),
        grid=(m // tm,),
        in_specs=[pl.BlockSpec((tm, d), row), _resident((1, d)), _resident(w.shape),
                  _resident((1, q_lora)), _resident((1, kv_lora)),
                  pl.BlockSpec((tm, rope), tab), pl.BlockSpec((tm, rope), tab)],
        out_specs=[pl.BlockSpec((tm, q_lora), row), pl.BlockSpec((tm, lat), row), pl.BlockSpec((tm, lat), row)],
        out_shape=[jax.ShapeDtypeStruct((m, q_lora), BF16), jax.ShapeDtypeStruct((m, lat), F32),
                   jax.ShapeDtypeStruct((m, lat), BF16)],
        compiler_params=_params("arbitrary"),
        name="mla_in",
    )(h, g, w, qn, kvn, cos2, sin2)


def _mla_q_kernel(cq_ref, wqb_ref, wkt_ref, cos_ref, sin_ref, q_ref, *, heads, nope, rope, kv_lora, scale):
    y = _dot(cq_ref[...], wqb_ref[...])
    cos = cos_ref[...]
    sin = sin_ref[...]
    pe0 = heads * nope
    rot0 = pe0 + heads * rope
    for h in range(heads):
        qn = y[:, h * nope:(h + 1) * nope].astype(BF16)
        q_ref[h, :, :kv_lora] = (_dot(qn, wkt_ref[h]) * scale).astype(BF16)
        pe = y[:, pe0 + h * rope:pe0 + (h + 1) * rope] * cos + y[:, rot0 + h * rope:rot0 + (h + 1) * rope] * sin
        q_ref[h, :, kv_lora:] = (pe * scale).astype(BF16)


def _mla_q(cq, wqb, wkt, cos2, sin2, *, scale):
    m, q_lora = cq.shape
    heads, nope, kv_lora = wkt.shape
    rope = cos2.shape[1]
    tm = min(256, m)
    row = lambda i: (i, 0)
    tab = _table_index(cos2.shape[0], tm)
    return pl.pallas_call(
        functools.partial(_mla_q_kernel, heads=heads, nope=nope, rope=rope, kv_lora=kv_lora, scale=scale),
        grid=(m // tm,),
        in_specs=[pl.BlockSpec((tm, q_lora), row), _resident(wqb.shape), _resident(wkt.shape),
                  pl.BlockSpec((tm, rope), tab), pl.BlockSpec((tm, rope), tab)],
        out_specs=pl.BlockSpec((heads, tm, kv_lora + rope), lambda i: (0, i, 0)),
        out_shape=jax.ShapeDtypeStruct((heads, m, kv_lora + rope), BF16),
        compiler_params=_params("arbitrary"),
        name="mla_q",
    )(cq, wqb, wkt, cos2, sin2)


def _softmax_step(sc, pv, m_sc, l_sc, acc_sc, r=slice(None)):
    m_prev = m_sc[r]
    m_new = jnp.maximum(m_prev, jnp.max(sc, axis=1, keepdims=True))
    alpha = jnp.exp(m_prev - m_new)
    p = jnp.exp(sc - m_new)
    l_sc[r] = alpha * l_sc[r] + jnp.sum(p, axis=1, keepdims=True)
    acc_sc[r] = alpha * acc_sc[r] + pv(p.astype(BF16))
    m_sc[r] = m_new


def _mla_prompt_kernel(qi_ref, kj_ref, q_ref, k_ref, o_ref, m_sc, l_sc, acc_sc, *, heads, tq, tk, kv_lora):
    s = pl.program_id(1)
    qi = qi_ref[s]
    kj = kj_ref[s]
    diag = (qi * tq + tq - 1) // tk
    hp = MLA_HEADS_PER_PASS
    rows = hp * tq

    @pl.when(kj == 0)
    def _():
        m_sc[...] = jnp.full_like(m_sc, NEG_INF)
        l_sc[...] = jnp.zeros_like(l_sc)
        acc_sc[...] = jnp.zeros_like(acc_sc)

    def step(mask):
        k = k_ref[...]
        v = k[:, :kv_lora]

        def scores(g):
            sc = _dot_nt(q_ref[g * hp:(g + 1) * hp].reshape(rows, q_ref.shape[-1]), k)
            return sc if mask is None else jnp.where(mask, sc, NEG_INF)

        n_groups = heads // hp
        sc = scores(0)
        for g in range(n_groups):
            nxt = scores(g + 1) if g + 1 < n_groups else None
            _softmax_step(sc, lambda p: _dot(p, v), m_sc, l_sc, acc_sc, slice(g * rows, (g + 1) * rows))
            sc = nxt

    @pl.when(kj != diag)
    def _():
        step(None)

    @pl.when(kj == diag)
    def _():
        q_pos = qi * tq + (lax.broadcasted_iota(jnp.int32, (rows, tk), 0) & (tq - 1))
        k_pos = kj * tk + lax.broadcasted_iota(jnp.int32, (rows, tk), 1)
        step(k_pos <= q_pos)
        o_ref[...] = (acc_sc[...] / l_sc[...]).astype(BF16).reshape(o_ref.shape)


def _mla_prompt_attn(q_full, rows_b, *, batch, seq, kv_lora):
    heads, _, lat = q_full.shape
    tq, tk = MLA_Q_TILE, MLA_K_TILE
    assert tq & (tq - 1) == 0 and seq % tq == 0 and seq % tk == 0
    nq, nk = seq // tq, seq // tk
    qi_list, kj_list = [], []
    for qi in range(nq):
        for kj in range((qi * tq + tq - 1) // tk + 1):
            qi_list.append(qi)
            kj_list.append(kj)
    qi_tab = jnp.asarray(qi_list, jnp.int32)
    kj_tab = jnp.asarray(kj_list, jnp.int32)
    rows = heads * tq
    grid_spec = pltpu.PrefetchScalarGridSpec(
        num_scalar_prefetch=2,
        grid=(batch, len(qi_list)),
        in_specs=[pl.BlockSpec((heads, tq, lat), lambda b, s, qi, kj: (0, b * nq + qi[s], 0)),
                  pl.BlockSpec((tk, lat), lambda b, s, qi, kj: (b * nk + kj[s], 0))],
        out_specs=pl.BlockSpec((heads, tq, kv_lora), lambda b, s, qi, kj: (0, b * nq + qi[s], 0)),
        scratch_shapes=[pltpu.VMEM((rows, 1), F32), pltpu.VMEM((rows, 1), F32), pltpu.VMEM((rows, kv_lora), F32)],
    )
    return pl.pallas_call(
        functools.partial(_mla_prompt_kernel, heads=heads, tq=tq, tk=tk, kv_lora=kv_lora),
        grid_spec=grid_spec,
        out_shape=jax.ShapeDtypeStruct((heads, batch * seq, kv_lora), BF16),
        compiler_params=_params("arbitrary", "arbitrary"),
        name="mla_prompt_attn",
    )(qi_tab, kj_tab, q_full, rows_b)


def _mla_sample_kernel(pt_ref, q_ref, new_ref, *refs, n_pages, dec_seq, kv_lora):
    page_refs = refs[:n_pages]
    o_ref, m_sc, l_sc, acc_sc = refs[n_pages:]
    g = pl.program_id(1)

    @pl.when(g == 0)
    def _():
        m_sc[...] = jnp.full_like(m_sc, NEG_INF)
        l_sc[...] = jnp.zeros_like(l_sc)
        acc_sc[...] = jnp.zeros_like(acc_sc)

    q = q_ref[...]
    chains = m_sc.shape[0]
    per = n_pages // chains
    kts, scs = [], []
    for c in range(chains):
        kts.append(jnp.concatenate([r[...].astype(BF16) for r in page_refs[c * per:(c + 1) * per]], axis=1))
        scs.append(_dot(q, kts[c]))
    for c in range(chains):
        vt = kts[c][:kv_lora]
        _softmax_step(scs[c], lambda p, vt=vt: _dot_nt(p, vt), m_sc, l_sc, acc_sc, c)

    @pl.when(g == pl.num_programs(1) - 1)
    def _():
        kn = new_ref[...]
        sc = _dot_nt(q, kn)
        t = lax.broadcasted_iota(jnp.int32, sc.shape, 0) % dec_seq
        c = lax.broadcasted_iota(jnp.int32, sc.shape, 1)
        _softmax_step(jnp.where(c <= t, sc, NEG_INF), lambda p: _dot(p, kn[:, :kv_lora]), m_sc, l_sc, acc_sc, 0)
        m = functools.reduce(jnp.maximum, [m_sc[c] for c in range(chains)])
        weights = [jnp.exp(m_sc[c] - m) for c in range(chains)]
        l = sum(l_sc[c] * weights[c] for c in range(chains))
        acc = sum(acc_sc[c] * weights[c] for c in range(chains))
        o_ref[...] = (acc / l).astype(BF16)


def _mla_sample_attn(q_s, new_pad, cache_t, layer, page_table, *, dec_seq, kv_lora):
    n_seq, rows, lat = q_s.shape
    n_pages_total = page_table.shape[1]
    p = PAGES_PER_STEP
    assert n_pages_total % p == 0 and p % SAMPLE_CHAINS == 0 and cache_t.shape[2:] == (lat, PAGE_SIZE)
    page_specs = [
        pl.BlockSpec((None, None, lat, PAGE_SIZE), lambda s, g, pt, j=j: (layer, pt[s, g * p + j], 0, 0))
        for j in range(p)
    ]
    grid_spec = pltpu.PrefetchScalarGridSpec(
        num_scalar_prefetch=1,
        grid=(n_seq, n_pages_total // p),
        in_specs=[pl.BlockSpec((None, rows, lat), lambda s, g, pt: (s, 0, 0)),
                  pl.BlockSpec((None, NEW_ROWS_PAD, lat), lambda s, g, pt: (s, 0, 0))] + page_specs,
        out_specs=pl.BlockSpec((None, rows, kv_lora), lambda s, g, pt: (s, 0, 0)),
        scratch_shapes=[pltpu.VMEM((SAMPLE_CHAINS, rows, 1), F32), pltpu.VMEM((SAMPLE_CHAINS, rows, 1), F32),
                        pltpu.VMEM((SAMPLE_CHAINS, rows, kv_lora), F32)],
    )
    return pl.pallas_call(
        functools.partial(_mla_sample_kernel, n_pages=p, dec_seq=dec_seq, kv_lora=kv_lora),
        grid_spec=grid_spec,
        out_shape=jax.ShapeDtypeStruct((n_seq, rows, kv_lora), BF16),
        compiler_params=_params("arbitrary", "arbitrary"),
        name="mla_sample_attn",
    )(page_table, q_s, new_pad, *([cache_t] * p))


def _mla_out_kernel(o_ref, wv_ref, wo_ref, h_ref, out_ref, *, heads):
    v = jnp.concatenate([_dot(o_ref[h], wv_ref[h]).astype(BF16) for h in range(heads)], axis=1)
    out_ref[...] = h_ref[...] + _dot(v, wo_ref[...])


def _mla_out(o_lat, wv, wo, h):
    heads, m, kv_lora = o_lat.shape
    d = h.shape[1]
    tm = min(256, m)
    row = lambda i: (i, 0)
    return pl.pallas_call(
        functools.partial(_mla_out_kernel, heads=heads),
        grid=(m // tm,),
        in_specs=[pl.BlockSpec((heads, tm, kv_lora), lambda i: (0, i, 0)), _resident(wv.shape), _resident(wo.shape),
                  pl.BlockSpec((tm, d), row)],
        out_specs=pl.BlockSpec((tm, d), row),
        out_shape=jax.ShapeDtypeStruct((m, d), F32),
        compiler_params=_params("arbitrary"),
        name="mla_out",
    )(o_lat, wv, wo, h)


def _ffn_kernel(h_ref, g_ref, wg_ref, wu_ref, wd_ref, fg_ref, out_ref, xn_sc, acc_sc, *, final_norm):
    j = pl.program_id(1)

    @pl.when(j == 0)
    def _():
        xn_sc[...] = _rms(h_ref[...], g_ref[...]).astype(BF16)
        acc_sc[...] = jnp.zeros_like(acc_sc)

    xn = xn_sc[...]
    chunk = wg_ref.shape[1] // FFN_CHUNKS

    def gate_up(c):
        cols = slice(c * chunk, (c + 1) * chunk)
        return _dot(xn, wg_ref[:, cols]), _dot(xn, wu_ref[:, cols])

    cur = gate_up(0)
    total = None
    for c in range(FFN_CHUNKS):
        nxt = gate_up(c + 1) if c + 1 < FFN_CHUNKS else None
        a, u = cur
        part = _dot((a * jax.nn.sigmoid(a) * u).astype(BF16), wd_ref[c * chunk:(c + 1) * chunk, :])
        total = part if total is None else total + part
        cur = nxt
    acc_sc[...] += total

    @pl.when(j == pl.num_programs(1) - 1)
    def _():
        res = h_ref[...] + acc_sc[...]
        out_ref[...] = _rms(res, fg_ref[...]) if final_norm else res


def _ffn(h, g, wg, wu, wd, fg, layer, *, final_norm):
    m, d = h.shape
    dff = wg.shape[2]
    tm = min(ROW_TILE, m)
    tf = FFN_TILE
    assert dff % tf == 0
    row = lambda i, j: (i, 0)
    return pl.pallas_call(
        functools.partial(_ffn_kernel, final_norm=final_norm),
        grid=(m // tm, dff // tf),
        in_specs=[pl.BlockSpec((tm, d), row), _resident((1, d)),
                  pl.BlockSpec((None, d, tf), lambda i, j: (layer, 0, j)),
                  pl.BlockSpec((None, d, tf), lambda i, j: (layer, 0, j)),
                  pl.BlockSpec((None, tf, d), lambda i, j: (layer, j, 0)), _resident((1, d))],
        out_specs=pl.BlockSpec((tm, d), row),
        out_shape=jax.ShapeDtypeStruct((m, d), F32),
        scratch_shapes=[pltpu.VMEM((tm, d), BF16), pltpu.VMEM((tm, d), F32)],
        compiler_params=_params("arbitrary", "arbitrary"),
        name="ffn",
    )(h, g, wg, wu, wd, fg)


def _swa_in_kernel(h_ref, ga_ref, gk_ref, wq_ref, bq_ref, wkv_ref, bkv_ref, q_ref, kv_ref, kvb_ref, *, scale):
    x = h_ref[...]
    y = x * lax.rsqrt(jnp.mean(x * x, axis=-1, keepdims=True) + EPS)
    xn = (y * ga_ref[...]).astype(BF16)
    hn = (y * gk_ref[...]).astype(BF16)
    q_ref[...] = ((_dot(xn, wq_ref[...]) + bq_ref[...]) * scale).astype(BF16)
    kv = _dot(hn, wkv_ref[...]) + bkv_ref[...]
    kv_ref[...] = kv[:, :kv_ref.shape[1]]
    kvb_ref[...] = kv.astype(BF16)


def _swa_in(h, ga, gk, wq, bq, wkv, bkv, *, scale):
    m, d = h.shape
    nq, nkv = wq.shape[1], wkv.shape[1]
    tm = min(256, m)
    row = lambda i: (i, 0)
    return pl.pallas_call(
        functools.partial(_swa_in_kernel, scale=scale),
        grid=(m // tm,),
        in_specs=[pl.BlockSpec((tm, d), row), _resident((1, d)), _resident((1, d)), _resident(wq.shape),
                  _resident((1, nq)), _resident(wkv.shape), _resident((1, nkv))],
        out_specs=[pl.BlockSpec((tm, nq), row), pl.BlockSpec((tm, nkv // 2), row), pl.BlockSpec((tm, nkv), row)],
        out_shape=[jax.ShapeDtypeStruct((m, nq), BF16), jax.ShapeDtypeStruct((m, nkv // 2), F32),
                   jax.ShapeDtypeStruct((m, nkv), BF16)],
        compiler_params=_params("arbitrary"),
        name="swa_in",
    )(h, ga, gk, wq, bq, wkv, bkv)


def _swa_prompt_kernel(q_ref, prev_ref, cur_ref, bias_ref, o_ref, *, n_kv, group, hd):
    i = pl.program_id(1)
    w = q_ref.shape[0]
    lanes = 2 * hd
    pairs = group // 2
    band = 2 * w
    q = q_ref[...]
    kv = jnp.concatenate([prev_ref[...], cur_ref[...]], axis=0)
    col = lax.broadcasted_iota(jnp.int32, (1, 2 * band), 1) & (band - 1)
    live = (col >= w) | (i > 0) | (col == 0)
    low = lax.broadcasted_iota(jnp.int32, (band, lanes), 1) < hd
    real = lax.broadcasted_iota(jnp.int32, (band, lanes), 0) > 0
    keep_first, keep_second = low & real, (~low) & real
    ones = jnp.concatenate([low, ~low], axis=0).astype(BF16)
    zero = jnp.zeros((), BF16)
    swapped = n_kv * lanes

    def block_diag(slab0, n):
        nat = kv[:, slab0 + (n // 2) * lanes:slab0 + (n // 2 + 1) * lanes]
        swp = kv[:, swapped + slab0 + (n // 2) * lanes:swapped + slab0 + (n // 2 + 1) * lanes]
        first, second = (nat, swp) if n % 2 == 0 else (swp, nat)
        return jnp.concatenate([jnp.where(keep_first, first, zero), jnp.where(keep_second, second, zero)], axis=0)

    for n in range(n_kv):
        k2 = block_diag(0, n)
        v2 = jnp.concatenate([block_diag(n_kv * hd, n), ones], axis=1)
        qn = jnp.concatenate([q[:, (n * pairs + p) * lanes:(n * pairs + p + 1) * lanes] for p in range(pairs)], axis=0)
        s = jnp.where(live, _dot_nt(qn, k2) + bias_ref[n], NEG_INF)
        probs = []
        for half in range(2):
            sh = s[:, half * band:(half + 1) * band]
            probs.append(jnp.exp(sh - jnp.max(sh, axis=1, keepdims=True)).astype(BF16))
        pv = _dot(jnp.concatenate(probs, axis=1), v2)
        o = pv[:, :lanes] / pv[:, lanes:]
        for p in range(pairs):
            o_ref[:, (n * pairs + p) * lanes:(n * pairs + p + 1) * lanes] = o[p * w:(p + 1) * w].astype(BF16)


def _swa_prompt_attn(q, kvb, bias, *, batch, seq, n_kv, group, hd):
    w = WINDOW
    nb = seq // w
    assert group % 2 == 0 and kvb.shape[1] == 4 * n_kv * hd
    return pl.pallas_call(
        functools.partial(_swa_prompt_kernel, n_kv=n_kv, group=group, hd=hd),
        grid=(batch, nb),
        in_specs=[pl.BlockSpec((w, q.shape[1]), lambda b, i: (b * nb + i, 0)),
                  pl.BlockSpec((w, kvb.shape[1]), lambda b, i: (b * nb + jnp.maximum(i - 1, 0), 0)),
                  pl.BlockSpec((w, kvb.shape[1]), lambda b, i: (b * nb + i, 0)),
                  _resident(bias.shape)],
        out_specs=pl.BlockSpec((w, q.shape[1]), lambda b, i: (b * nb + i, 0)),
        out_shape=jax.ShapeDtypeStruct(q.shape, BF16),
        compiler_params=_params("arbitrary", "arbitrary"),
        name="swa_prompt_attn",
    )(q, kvb, kvb, bias)


def _swa_sample_kernel(q_ref, state_ref, new_ref, bias_ref, sink_ref, o_ref, win_ref, *, n_kv, hd, dec_seq):
    n_win = state_ref.shape[2]
    lane = lax.broadcasted_iota(jnp.int32, (1, n_win), 1)
    for sb in range(q_ref.shape[0]):
        state_t = state_ref[sb]
        new = new_ref[sb]
        lead = jnp.zeros((n_win - new.shape[0], new.shape[1]), F32)
        new_t = jnp.concatenate([lead, new], axis=0).T
        win_ref[sb] = jnp.where(lane < n_win - dec_seq, pltpu.roll(state_t, n_win - dec_seq, axis=1), new_t)
        kv_t = jnp.concatenate([state_t, new_t], axis=1).astype(BF16)
        for n in range(n_kv):
            kt = kv_t[n * hd:(n + 1) * hd]
            vt = kv_t[(n_kv + n) * hd:(n_kv + n + 1) * hd]
            s = _dot(q_ref[sb, n], kt) + bias_ref[n]
            sink = sink_ref[n]
            m = jnp.maximum(jnp.max(s, axis=1, keepdims=True), sink)
            p = jnp.exp(s - m)
            denom = jnp.sum(p, axis=1, keepdims=True) + jnp.exp(sink - m)
            o_ref[sb, n] = (_dot_nt(p.astype(BF16), vt) / denom).astype(BF16)


def _swa_sample_attn(q4, state_t, new_pad, bias, sink_rows, *, dec_seq):
    n_seq, n_kv, rows, hd = q4.shape
    width, n_win = state_t.shape[1], state_t.shape[2]
    sb = SWA_SEQS_PER_STEP
    assert n_seq % sb == 0 and bias.shape[2] == 2 * n_win
    blk = lambda i: (i, 0, 0)
    return pl.pallas_call(
        functools.partial(_swa_sample_kernel, n_kv=n_kv, hd=hd, dec_seq=dec_seq),
        grid=(n_seq // sb,),
        in_specs=[pl.BlockSpec((sb, n_kv, rows, hd), lambda i: (i, 0, 0, 0)),
                  pl.BlockSpec((sb, width, n_win), blk), pl.BlockSpec((sb,) + new_pad.shape[1:], blk),
                  _resident(bias.shape), _resident(sink_rows.shape)],
        out_specs=[pl.BlockSpec((sb, n_kv, rows, hd), lambda i: (i, 0, 0, 0)), pl.BlockSpec((sb, width, n_win), blk)],
        out_shape=[jax.ShapeDtypeStruct(q4.shape, BF16), jax.ShapeDtypeStruct(state_t.shape, F32)],
        compiler_params=_params("arbitrary"),
        name="swa_sample_attn",
    )(q4, state_t, new_pad, bias, sink_rows)


def _proj_residual_kernel(o_ref, w_ref, b_ref, h_ref, out_ref):
    out_ref[...] = h_ref[...] + (_dot(o_ref[...], w_ref[...]) + b_ref[...])


def _proj_residual(o, w, b, h):
    m, d = h.shape
    tm = min(ROW_TILE, m)
    row = lambda i: (i, 0)
    return pl.pallas_call(
        _proj_residual_kernel,
        grid=(m // tm,),
        in_specs=[pl.BlockSpec((tm, o.shape[1]), row), _resident(w.shape), _resident((1, d)), pl.BlockSpec((tm, d), row)],
        out_specs=pl.BlockSpec((tm, d), row),
        out_shape=jax.ShapeDtypeStruct((m, d), F32),
        compiler_params=_params("arbitrary"),
        name="swa_out",
    )(o, w, b, h)


def _rope_tables(pos, rope):
    inv_freq = ROPE_THETA ** (-jnp.arange(0, rope, 2, dtype=F32) / rope)
    ang = pos.astype(F32)[:, None] * inv_freq[None, :]
    cos, sin = jnp.cos(ang), jnp.sin(ang)
    return jnp.concatenate([cos, cos], axis=1), jnp.concatenate([sin, sin], axis=1)


def _rot_half_cols(w):
    half = w.shape[-1] // 2
    return jnp.concatenate([-w[..., half:], w[..., :half]], axis=-1)


def _bucket(dist):
    n = jnp.maximum(dist, 0)
    max_exact = N_BUCKETS // 2
    nf = jnp.maximum(n, 1).astype(F32)
    large = max_exact + (jnp.log(nf / max_exact) / math.log(MAX_DISTANCE / max_exact)
                         * (N_BUCKETS - max_exact)).astype(jnp.int32)
    return jnp.where(n < max_exact, n, jnp.minimum(large, N_BUCKETS - 1))


def _window_bias(rel_bias, dist):
    valid = (dist >= 0) & (dist < WINDOW)
    bias = jnp.moveaxis(jnp.take(rel_bias.astype(F32), _bucket(dist), axis=0), -1, 0)
    return jnp.where(valid[None], bias, NEG_INF)


def _band_bias(rel_bias):
    w = WINDOW
    per_dist = _window_bias(rel_bias, jnp.arange(w, dtype=jnp.int32)[None, :])[:, 0, :]
    heads = per_dist.shape[0]
    neg = lambda n: jnp.full((heads, n), NEG_INF, F32)
    line = jnp.concatenate([neg(w), per_dist[:, ::-1], neg(w)], axis=1)
    skew = jnp.tile(line, (1, w))[:, :w * (3 * w - 1)].reshape(heads, w, 3 * w - 1)
    return skew[:, :, w - 1:3 * w - 1]


def kernel(x_prompt, x_sample, cache_mla, state_kv_win, page_table, attn_norm, ffn_norm, final_norm, mla_wq_a, mla_q_norm, mla_wq_b, mla_wkv_a, mla_kv_norm, mla_wkv_b, mla_wo, kv_norm_shared, w_k_shared, b_k_shared, w_v_shared, b_v_shared, swa_wq, swa_bq, swa_sinks, swa_wo, swa_bo, rel_bias, ffn_w_gate, ffn_w_up, ffn_w_down):
    batch, seq, d = x_prompt.shape
    dec_batch, dec_seq, _ = x_sample.shape
    depth = attn_norm.shape[0]
    n_a = mla_wq_a.shape[0]
    past_len = page_table.shape[1] * PAGE_SIZE
    n_win = state_kv_win.shape[1]
    q_lora = mla_wq_a.shape[2]
    kv_lora = mla_kv_norm.shape[1]
    lat = mla_wkv_a.shape[2]
    rope = lat - kv_lora
    heads_a = mla_wq_b.shape[2]
    nope = mla_wq_b.shape[3] - rope
    heads_b, hd = swa_wq.shape[2], swa_wq.shape[3]
    n_kv = w_k_shared.shape[1]
    group = heads_b // n_kv
    assert seq % WINDOW == 0 and n_win == WINDOW and dec_seq <= 8
    mla_scale = 1.0 / math.sqrt(nope + rope)
    swa_scale = 1.0 / math.sqrt(hd)

    vec = lambda a: a.reshape(1, -1).astype(F32)
    streams = [x_prompt.reshape(batch * seq, d), x_sample.reshape(dec_batch * dec_seq, d)]
    rope_tabs = [_rope_tables(jnp.arange(seq, dtype=jnp.int32), rope),
                 _rope_tables(jnp.tile(past_len + jnp.arange(dec_seq, dtype=jnp.int32), dec_batch), rope)]

    cache_t = jnp.swapaxes(cache_mla, 2, 3)
    ffn_wg, ffn_wu, ffn_wd = ffn_w_gate.astype(BF16), ffn_w_up.astype(BF16), ffn_w_down.astype(BF16)

    rows_p_all, rows_s_all = [], []
    kv_p = kv_win_sample = None
    for layer in range(depth):
        ga = vec(attn_norm[layer])
        if layer < n_a:
            a = layer
            wkv_a = mla_wkv_a[a]
            w_in = jnp.concatenate([mla_wq_a[a], wkv_a, _rot_half_cols(wkv_a[:, kv_lora:])], axis=1).astype(BF16)
            wq_pe = mla_wq_b[a][:, :, nope:]
            wqb = jnp.concatenate([mla_wq_b[a][:, :, :nope].reshape(q_lora, heads_a * nope),
                                   wq_pe.reshape(q_lora, heads_a * rope),
                                   _rot_half_cols(wq_pe).reshape(q_lora, heads_a * rope)], axis=1).astype(BF16)
            wkt = jnp.transpose(mla_wkv_b[a][:, :, :nope], (1, 2, 0)).astype(BF16)
            wv = jnp.transpose(mla_wkv_b[a][:, :, nope:], (1, 0, 2)).astype(BF16)
            wo = mla_wo[a].reshape(-1, d).astype(BF16)

            stage = []
            for h, (cos2, sin2) in zip(streams, rope_tabs):
                cq, rows, rows_b = _mla_in(h, ga, w_in, vec(mla_q_norm[a]), vec(mla_kv_norm[a]), cos2, sin2,
                                           q_lora=q_lora, kv_lora=kv_lora, rope=rope)
                stage.append((_mla_q(cq, wqb, wkt, cos2, sin2, scale=mla_scale), rows, rows_b))
            (q_p, rows_p, rows_pb), (q_s, rows_s, rows_sb) = stage
            rows_p_all.append(rows_p.reshape(batch, seq, lat))
            rows_s_all.append(rows_s.reshape(dec_batch, dec_seq, lat))

            o_p = _mla_prompt_attn(q_p, rows_pb, batch=batch, seq=seq, kv_lora=kv_lora)
            q_s = q_s.reshape(heads_a, dec_batch, dec_seq, lat).transpose(1, 0, 2, 3).reshape(dec_batch, heads_a * dec_seq, lat)
            new_pad = jnp.pad(rows_sb.reshape(dec_batch, dec_seq, lat), ((0, 0), (0, NEW_ROWS_PAD - dec_seq), (0, 0)))
            o_s = _mla_sample_attn(q_s, new_pad, cache_t, a, page_table, dec_seq=dec_seq, kv_lora=kv_lora)
            o_s = o_s.reshape(dec_batch, heads_a, dec_seq, kv_lora).transpose(1, 0, 2, 3).reshape(heads_a, dec_batch * dec_seq, kv_lora)
            streams = [_mla_out(o, wv, wo, h) for o, h in zip((o_p, o_s), streams)]
        else:
            b = layer - n_a
            wq = swa_wq[b].reshape(d, heads_b * hd).astype(BF16)
            swap_pairs = lambda t: t.reshape(t.shape[0], n_kv // 2, 2, hd)[:, :, ::-1].reshape(t.shape[0], n_kv * hd)
            wk, wv_s = w_k_shared.reshape(d, n_kv * hd), w_v_shared.reshape(d, n_kv * hd)
            bk, bv = b_k_shared.reshape(1, -1), b_v_shared.reshape(1, -1)
            wkv = jnp.concatenate([wk, wv_s, swap_pairs(wk), swap_pairs(wv_s)], axis=1).astype(BF16)
            bkv = jnp.concatenate([bk, bv, swap_pairs(bk), swap_pairs(bv)], axis=1).astype(F32)
            (q_p, kv_p_new, kvb_p), (q_s, kv_s_new, _) = [
                _swa_in(h, ga, vec(kv_norm_shared), wq, vec(swa_bq[b]), wkv, bkv, scale=swa_scale) for h in streams]
            if b == 0:
                kv_p, kv_pb, kv_s = kv_p_new, kvb_p, kv_s_new
                state = jnp.transpose(state_kv_win, (0, 2, 3, 4, 1)).reshape(dec_batch, 2 * n_kv * hd, n_win)
            sinks = swa_sinks[b].astype(F32)

            bias_p = _band_bias(rel_bias).at[:, :, 0].set(sinks[:, None])
            pairs = group // 2
            bias_p = bias_p.reshape(n_kv, pairs, 2, WINDOW, 2 * WINDOW)
            bias_p = bias_p.transpose(0, 1, 3, 2, 4).reshape(n_kv, pairs * WINDOW, 4 * WINDOW)
            o_p = _swa_prompt_attn(q_p, kv_pb, bias_p, batch=batch, seq=seq, n_kv=n_kv, group=group, hd=hd)

            n_keys = 2 * n_win
            t_idx = jnp.arange(dec_seq, dtype=jnp.int32)[:, None]
            j_idx = jnp.arange(n_keys, dtype=jnp.int32)[None, :]
            c_idx = jnp.where(j_idx < n_win, j_idx, j_idx - (n_win - dec_seq))
            real = (j_idx < n_win) | (j_idx >= n_keys - dec_seq)
            dist_s = jnp.where(real, n_win + t_idx - c_idx, -1)
            bias_s = _window_bias(rel_bias, dist_s)
            bias_s = bias_s.reshape(n_kv, group * dec_seq, n_keys)
            sink_s = jnp.repeat(sinks, dec_seq).reshape(n_kv, group * dec_seq, 1)
            q4 = q_s.reshape(dec_batch, dec_seq, n_kv, group, hd).transpose(0, 2, 3, 1, 4).reshape(dec_batch, n_kv, group * dec_seq, hd)
            new_pad = jnp.pad(kv_s.reshape(dec_batch, dec_seq, -1), ((0, 0), (8 - dec_seq, 0), (0, 0)))
            o4, win = _swa_sample_attn(q4, state, new_pad, bias_s, sink_s, dec_seq=dec_seq)
            if b == 0:
                kv_win_sample = win
            o_s = o4.reshape(dec_batch, n_kv, group, dec_seq, hd).transpose(0, 3, 1, 2, 4).reshape(dec_batch * dec_seq, heads_b * hd)
            wo = swa_wo[b].reshape(heads_b * hd, d).astype(BF16)
            streams = [_proj_residual(o, wo, vec(swa_bo[b]), h) for o, h in zip((o_p, o_s), streams)]

        last = layer == depth - 1
        streams = [_ffn(h, vec(ffn_norm[layer]), ffn_wg, ffn_wu, ffn_wd, vec(final_norm), layer, final_norm=last)
                   for h in streams]

    y_prompt = streams[0].reshape(batch, seq, d)
    y_sample = streams[1].reshape(dec_batch, dec_seq, d)
    n_tail = min(WINDOW, seq)
    kv_win_prompt = kv_p.reshape(batch, seq, -1)[:, seq - n_tail:].reshape(batch, n_tail, 2, n_kv, hd)
    kv_win_sample = jnp.transpose(kv_win_sample.reshape(dec_batch, 2, n_kv, hd, n_win), (0, 4, 1, 2, 3))
    return (y_prompt, y_sample, jnp.stack(rows_p_all, axis=0), jnp.stack(rows_s_all, axis=0),
            kv_win_prompt, kv_win_sample)
```

```python
import functools
import math

import jax
import jax.numpy as jnp
from jax import lax
from jax.experimental import pallas as pl
from jax.experimental.pallas import tpu as pltpu

F32 = jnp.float32
BF16 = jnp.bfloat16

EPS = 1e-6
NEG_INF = -1e30
ROPE_THETA = 10000.0
PAGE_SIZE = 128
WINDOW = 128
N_BUCKETS = 32
MAX_DISTANCE = 128

VMEM_LIMIT_BYTES = 52 * 1024 * 1024
ROW_TILE = 512
FFN_TILE = 512
FFN_CHUNKS = 2
MLA_Q_TILE = 256
MLA_K_TILE = 512
MLA_HEADS_PER_PASS = 2
PAGES_PER_STEP = 32
SAMPLE_CHAINS = 2
PAGE_RING_STEPS = 2
NEW_ROWS_PAD = 16
SWA_SEQS_PER_STEP = 8


def _params(*semantics):
    return pltpu.CompilerParams(dimension_semantics=semantics, vmem_limit_bytes=VMEM_LIMIT_BYTES)


def _rms(x, g):
    return x * lax.rsqrt(jnp.mean(x * x, axis=-1, keepdims=True) + EPS) * g


def _dot(a, b):
    return jnp.dot(a, b, preferred_element_type=F32)


def _dot_nt(a, b):
    return lax.dot_general(a, b, (((1,), (1,)), ((), ())), preferred_element_type=F32)


def _resident(shape):
    return pl.BlockSpec(shape, lambda *_: (0,) * len(shape))


def _table_index(table_rows, tm):
    assert table_rows % tm == 0
    blocks = table_rows // tm
    return lambda i: (i % blocks, 0)


def _mla_in_kernel(h_ref, g_ref, w_ref, qn_ref, kvn_ref, cos_ref, sin_ref,
                   cq_ref, rows_ref, rowsb_ref, *, q_lora, kv_lora, rope):
    xn = _rms(h_ref[...], g_ref[...]).astype(BF16)
    y = _dot(xn, w_ref[...])
    cq_ref[...] = _rms(y[:, :q_lora], qn_ref[...]).astype(BF16)
    c0 = q_lora
    ckv = _rms(y[:, c0:c0 + kv_lora], kvn_ref[...])
    p0 = c0 + kv_lora
    kpe = y[:, p0:p0 + rope] * cos_ref[...] + y[:, p0 + rope:p0 + 2 * rope] * sin_ref[...]
    rows_ref[:, :kv_lora] = ckv
    rows_ref[:, kv_lora:] = kpe
    rowsb_ref[:, :kv_lora] = ckv.astype(BF16)
    rowsb_ref[:, kv_lora:] = kpe.astype(BF16)


def _mla_in(h, g, w, qn, kvn, cos2, sin2, *, q_lora, kv_lora, rope):
    m, d = h.shape
    tm = min(ROW_TILE, m)
    lat = kv_lora + rope
    row = lambda i: (i, 0)
    tab = _table_index(cos2.shape[0], tm)
    return pl.pallas_call(
        functools.partial(_mla_in_kernel, q_lora=q_lora, kv_lora=kv_lora, rope=rope),
        grid=(m // tm,),
        in_specs=[pl.BlockSpec((tm, d), row), _resident((1, d)), _resident(w.shape),
                  _resident((1, q_lora)), _resident((1, kv_lora)),
                  pl.BlockSpec((tm, rope), tab), pl.BlockSpec((tm, rope), tab)],
        out_specs=[pl.BlockSpec((tm, q_lora), row), pl.BlockSpec((tm, lat), row), pl.BlockSpec((tm, lat), row)],
        out_shape=[jax.ShapeDtypeStruct((m, q_lora), BF16), jax.ShapeDtypeStruct((m, lat), F32),
                   jax.ShapeDtypeStruct((m, lat), BF16)],
        compiler_params=_params("arbitrary"),
        name="mla_in",
    )(h, g, w, qn, kvn, cos2, sin2)


def _mla_q_kernel(cq_ref, wqb_ref, wkt_ref, cos_ref, sin_ref, q_ref, *, heads, nope, rope, kv_lora, scale):
    y = _dot(cq_ref[...], wqb_ref[...])
    cos = cos_ref[...]
    sin = sin_ref[...]
    pe0 = heads * nope
    rot0 = pe0 + heads * rope
    for h in range(heads):
        qn = y[:, h * nope:(h + 1) * nope].astype(BF16)
        q_ref[h, :, :kv_lora] = (_dot(qn, wkt_ref[h]) * scale).astype(BF16)
        pe = y[:, pe0 + h * rope:pe0 + (h + 1) * rope] * cos + y[:, rot0 + h * rope:rot0 + (h + 1) * rope] * sin
        q_ref[h, :, kv_lora:] = (pe * scale).astype(BF16)


def _mla_q(cq, wqb, wkt, cos2, sin2, *, scale):
    m, q_lora = cq.shape
    heads, nope, kv_lora = wkt.shape
    rope = cos2.shape[1]
    tm = min(256, m)
    row = lambda i: (i, 0)
    tab = _table_index(cos2.shape[0], tm)
    return pl.pallas_call(
        functools.partial(_mla_q_kernel, heads=heads, nope=nope, rope=rope, kv_lora=kv_lora, scale=scale),
        grid=(m // tm,),
        in_specs=[pl.BlockSpec((tm, q_lora), row), _resident(wqb.shape), _resident(wkt.shape),
                  pl.BlockSpec((tm, rope), tab), pl.BlockSpec((tm, rope), tab)],
        out_specs=pl.BlockSpec((heads, tm, kv_lora + rope), lambda i: (0, i, 0)),
        out_shape=jax.ShapeDtypeStruct((heads, m, kv_lora + rope), BF16),
        compiler_params=_params("arbitrary"),
        name="mla_q",
    )(cq, wqb, wkt, cos2, sin2)


def _softmax_step(sc, pv, m_sc, l_sc, acc_sc, r=slice(None)):
    m_prev = m_sc[r]
    m_new = jnp.maximum(m_prev, jnp.max(sc, axis=1, keepdims=True))
    alpha = jnp.exp(m_prev - m_new)
    p = jnp.exp(sc - m_new)
    l_sc[r] = alpha * l_sc[r] + jnp.sum(p, axis=1, keepdims=True)
    acc_sc[r] = alpha * acc_sc[r] + pv(p.astype(BF16))
    m_sc[r] = m_new


def _mla_prompt_kernel(qi_ref, kj_ref, q_ref, k_ref, o_ref, m_sc, l_sc, acc_sc, *, heads, tq, tk, kv_lora):
    s = pl.program_id(1)
    qi = qi_ref[s]
    kj = kj_ref[s]
    diag = (qi * tq + tq - 1) // tk
    hp = MLA_HEADS_PER_PASS
    rows = hp * tq

    @pl.when(kj == 0)
    def _():
        m_sc[...] = jnp.full_like(m_sc, NEG_INF)
        l_sc[...] = jnp.zeros_like(l_sc)
        acc_sc[...] = jnp.zeros_like(acc_sc)

    def step(mask):
        k = k_ref[...]
        v = k[:, :kv_lora]

        def scores(g):
            sc = _dot_nt(q_ref[g * hp:(g + 1) * hp].reshape(rows, q_ref.shape[-1]), k)
            return sc if mask is None else jnp.where(mask, sc, NEG_INF)

        n_groups = heads // hp
        sc = scores(0)
        for g in range(n_groups):
            nxt = scores(g + 1) if g + 1 < n_groups else None
            _softmax_step(sc, lambda p: _dot(p, v), m_sc, l_sc, acc_sc, slice(g * rows, (g + 1) * rows))
            sc = nxt

    @pl.when(kj != diag)
    def _():
        step(None)

    @pl.when(kj == diag)
    def _():
        q_pos = qi * tq + (lax.broadcasted_iota(jnp.int32, (rows, tk), 0) & (tq - 1))
        k_pos = kj * tk + lax.broadcasted_iota(jnp.int32, (rows, tk), 1)
        step(k_pos <= q_pos)
        o_ref[...] = (acc_sc[...] / l_sc[...]).astype(BF16).reshape(o_ref.shape)


def _mla_prompt_attn(q_full, rows_b, *, batch, seq, kv_lora):
    heads, _, lat = q_full.shape
    tq, tk = MLA_Q_TILE, MLA_K_TILE
    assert tq & (tq - 1) == 0 and seq % tq == 0 and seq % tk == 0
    nq, nk = seq // tq, seq // tk
    qi_list, kj_list = [], []
    for qi in range(nq):
        for kj in range((qi * tq + tq - 1) // tk + 1):
            qi_list.append(qi)
            kj_list.append(kj)
    qi_tab = jnp.asarray(qi_list, jnp.int32)
    kj_tab = jnp.asarray(kj_list, jnp.int32)
    rows = heads * tq
    grid_spec = pltpu.PrefetchScalarGridSpec(
        num_scalar_prefetch=2,
        grid=(batch, len(qi_list)),
        in_specs=[pl.BlockSpec((heads, tq, lat), lambda b, s, qi, kj: (0, b * nq + qi[s], 0)),
                  pl.BlockSpec((tk, lat), lambda b, s, qi, kj: (b * nk + kj[s], 0))],
        out_specs=pl.BlockSpec((heads, tq, kv_lora), lambda b, s, qi, kj: (0, b * nq + qi[s], 0)),
        scratch_shapes=[pltpu.VMEM((rows, 1), F32), pltpu.VMEM((rows, 1), F32), pltpu.VMEM((rows, kv_lora), F32)],
    )
    return pl.pallas_call(
        functools.partial(_mla_prompt_kernel, heads=heads, tq=tq, tk=tk, kv_lora=kv_lora),
        grid_spec=grid_spec,
        out_shape=jax.ShapeDtypeStruct((heads, batch * seq, kv_lora), BF16),
        compiler_params=_params("arbitrary", "arbitrary"),
        name="mla_prompt_attn",
    )(qi_tab, kj_tab, q_full, rows_b)


def _mla_sample_kernel(pt_ref, q_ref, new_ref, cache_ref, o_ref, ring, sems, kt_sc, m_sc, l_sc, acc_sc, *,
                       layer, n_pages, dec_seq, kv_lora):
    s, g = pl.program_id(0), pl.program_id(1)
    steps = pl.num_programs(1)
    ring_size = ring.shape[0]
    per_seq = pt_ref.shape[1]
    total = pt_ref.shape[0] * per_seq
    base = (s * steps + g) * n_pages
    slot0 = (g % (ring_size // n_pages)) * n_pages

    def page_copy(n, slot):
        page = pt_ref[lax.shift_right_logical(n, per_seq.bit_length() - 1), n & (per_seq - 1)]
        return pltpu.make_async_copy(cache_ref.at[layer, page], ring.at[slot], sems.at[slot])

    @pl.when((s == 0) & (g == 0))
    def _():
        for j in range(ring_size):
            page_copy(j, j).start()

    @pl.when(g == 0)
    def _():
        m_sc[...] = jnp.full_like(m_sc, NEG_INF)
        l_sc[...] = jnp.zeros_like(l_sc)
        acc_sc[...] = jnp.zeros_like(acc_sc)

    chains = m_sc.shape[0]
    per = n_pages // chains
    width = ring.shape[2]
    for j in range(n_pages):
        page_copy(base + j, slot0 + j).wait()
        kt_sc[j // per, :, (j % per) * width:(j % per + 1) * width] = ring[slot0 + j].astype(BF16)

    @pl.when(base + ring_size < total)
    def _():
        for j in range(n_pages):
            page_copy(base + ring_size + j, slot0 + j).start()

    q = q_ref[...]
    scs = [_dot(q, kt_sc[c]) for c in range(chains)]
    for c in range(chains):
        _softmax_step(scs[c], lambda p, c=c: _dot_nt(p, kt_sc[c, :kv_lora]), m_sc, l_sc, acc_sc, c)

    @pl.when(g == pl.num_programs(1) - 1)
    def _():
        kn = new_ref[...]
        sc = _dot_nt(q, kn)
        t = lax.broadcasted_iota(jnp.int32, sc.shape, 0) % dec_seq
        c = lax.broadcasted_iota(jnp.int32, sc.shape, 1)
        _softmax_step(jnp.where(c <= t, sc, NEG_INF), lambda p: _dot(p, kn[:, :kv_lora]), m_sc, l_sc, acc_sc, 0)
        m = functools.reduce(jnp.maximum, [m_sc[c] for c in range(chains)])
        weights = [jnp.exp(m_sc[c] - m) for c in range(chains)]
        l = sum(l_sc[c] * weights[c] for c in range(chains))
        acc = sum(acc_sc[c] * weights[c] for c in range(chains))
        o_ref[...] = (acc / l).astype(BF16)


def _mla_sample_attn(q_s, new_pad, cache_t, layer, page_table, *, dec_seq, kv_lora):
    n_seq, rows, lat = q_s.shape
    per_seq = page_table.shape[1]
    p = PAGES_PER_STEP
    ring_size = PAGE_RING_STEPS * p
    assert per_seq & (per_seq - 1) == 0 and per_seq % ring_size == 0 and p % SAMPLE_CHAINS == 0
    assert cache_t.shape[2:] == (lat, PAGE_SIZE)
    grid_spec = pltpu.PrefetchScalarGridSpec(
        num_scalar_prefetch=1,
        grid=(n_seq, per_seq // p),
        in_specs=[pl.BlockSpec((None, rows, lat), lambda s, g, pt: (s, 0, 0)),
                  pl.BlockSpec((None, NEW_ROWS_PAD, lat), lambda s, g, pt: (s, 0, 0)),
                  pl.BlockSpec(memory_space=pl.ANY)],
        out_specs=pl.BlockSpec((None, rows, kv_lora), lambda s, g, pt: (s, 0, 0)),
        scratch_shapes=[pltpu.VMEM((ring_size, lat, PAGE_SIZE), F32), pltpu.SemaphoreType.DMA((ring_size,)),
                        pltpu.VMEM((SAMPLE_CHAINS, lat, p // SAMPLE_CHAINS * PAGE_SIZE), BF16),
                        pltpu.VMEM((SAMPLE_CHAINS, rows, 1), F32), pltpu.VMEM((SAMPLE_CHAINS, rows, 1), F32),
                        pltpu.VMEM((SAMPLE_CHAINS, rows, kv_lora), F32)],
    )
    return pl.pallas_call(
        functools.partial(_mla_sample_kernel, layer=layer, n_pages=p, dec_seq=dec_seq, kv_lora=kv_lora),
        grid_spec=grid_spec,
        out_shape=jax.ShapeDtypeStruct((n_seq, rows, kv_lora), BF16),
        compiler_params=_params("arbitrary", "arbitrary"),
        name="mla_sample_attn",
    )(page_table, q_s, new_pad, cache_t)


def _mla_out_kernel(o_ref, wv_ref, wo_ref, h_ref, out_ref, *, heads):
    v = jnp.concatenate([_dot(o_ref[h], wv_ref[h]).astype(BF16) for h in range(heads)], axis=1)
    out_ref[...] = h_ref[...] + _dot(v, wo_ref[...])


def _mla_out(o_lat, wv, wo, h):
    heads, m, kv_lora = o_lat.shape
    d = h.shape[1]
    tm = min(256, m)
    row = lambda i: (i, 0)
    return pl.pallas_call(
        functools.partial(_mla_out_kernel, heads=heads),
        grid=(m // tm,),
        in_specs=[pl.BlockSpec((heads, tm, kv_lora), lambda i: (0, i, 0)), _resident(wv.shape), _resident(wo.shape),
                  pl.BlockSpec((tm, d), row)],
        out_specs=pl.BlockSpec((tm, d), row),
        out_shape=jax.ShapeDtypeStruct((m, d), F32),
        compiler_params=_params("arbitrary"),
        name="mla_out",
    )(o_lat, wv, wo, h)


def _ffn_kernel(h_ref, g_ref, wg_ref, wu_ref, wd_ref, fg_ref, out_ref, xn_sc, acc_sc, *, final_norm):
    j = pl.program_id(1)

    @pl.when(j == 0)
    def _():
        xn_sc[...] = _rms(h_ref[...], g_ref[...]).astype(BF16)
        acc_sc[...] = jnp.zeros_like(acc_sc)

    xn = xn_sc[...]
    chunk = wg_ref.shape[1] // FFN_CHUNKS

    def gate_up(c):
        cols = slice(c * chunk, (c + 1) * chunk)
        return _dot(xn, wg_ref[:, cols]), _dot(xn, wu_ref[:, cols])

    cur = gate_up(0)
    total = None
    for c in range(FFN_CHUNKS):
        nxt = gate_up(c + 1) if c + 1 < FFN_CHUNKS else None
        a, u = cur
        part = _dot((a * jax.nn.sigmoid(a) * u).astype(BF16), wd_ref[c * chunk:(c + 1) * chunk, :])
        total = part if total is None else total + part
        cur = nxt
    acc_sc[...] += total

    @pl.when(j == pl.num_programs(1) - 1)
    def _():
        res = h_ref[...] + acc_sc[...]
        out_ref[...] = _rms(res, fg_ref[...]) if final_norm else res


def _ffn(h, g, wg, wu, wd, fg, layer, *, final_norm):
    m, d = h.shape
    dff = wg.shape[2]
    tm = min(ROW_TILE, m)
    tf = FFN_TILE
    assert dff % tf == 0
    row = lambda i, j: (i, 0)
    return pl.pallas_call(
        functools.partial(_ffn_kernel, final_norm=final_norm),
        grid=(m // tm, dff // tf),
        in_specs=[pl.BlockSpec((tm, d), row), _resident((1, d)),
                  pl.BlockSpec((None, d, tf), lambda i, j: (layer, 0, j)),
                  pl.BlockSpec((None, d, tf), lambda i, j: (layer, 0, j)),
                  pl.BlockSpec((None, tf, d), lambda i, j: (layer, j, 0)), _resident((1, d))],
        out_specs=pl.BlockSpec((tm, d), row),
        out_shape=jax.ShapeDtypeStruct((m, d), F32),
        scratch_shapes=[pltpu.VMEM((tm, d), BF16), pltpu.VMEM((tm, d), F32)],
        compiler_params=_params("arbitrary", "arbitrary"),
        name="ffn",
    )(h, g, wg, wu, wd, fg)


def _swa_in_kernel(h_ref, ga_ref, gk_ref, wq_ref, bq_ref, wkv_ref, bkv_ref, q_ref, kv_ref, kvb_ref, *, scale):
    x = h_ref[...]
    y = x * lax.rsqrt(jnp.mean(x * x, axis=-1, keepdims=True) + EPS)
    xn = (y * ga_ref[...]).astype(BF16)
    hn = (y * gk_ref[...]).astype(BF16)
    q_ref[...] = ((_dot(xn, wq_ref[...]) + bq_ref[...]) * scale).astype(BF16)
    kv = _dot(hn, wkv_ref[...]) + bkv_ref[...]
    kv_ref[...] = kv[:, :kv_ref.shape[1]]
    kvb_ref[...] = kv.astype(BF16)


def _swa_in(h, ga, gk, wq, bq, wkv, bkv, *, scale):
    m, d = h.shape
    nq, nkv = wq.shape[1], wkv.shape[1]
    tm = min(256, m)
    row = lambda i: (i, 0)
    return pl.pallas_call(
        functools.partial(_swa_in_kernel, scale=scale),
        grid=(m // tm,),
        in_specs=[pl.BlockSpec((tm, d), row), _resident((1, d)), _resident((1, d)), _resident(wq.shape),
                  _resident((1, nq)), _resident(wkv.shape), _resident((1, nkv))],
        out_specs=[pl.BlockSpec((tm, nq), row), pl.BlockSpec((tm, nkv // 2), row), pl.BlockSpec((tm, nkv), row)],
        out_shape=[jax.ShapeDtypeStruct((m, nq), BF16), jax.ShapeDtypeStruct((m, nkv // 2), F32),
                   jax.ShapeDtypeStruct((m, nkv), BF16)],
        compiler_params=_params("arbitrary"),
        name="swa_in",
    )(h, ga, gk, wq, bq, wkv, bkv)


def _swa_prompt_kernel(q_ref, prev_ref, cur_ref, bias_ref, o_ref, *, n_kv, group, hd):
    i = pl.program_id(1)
    w = q_ref.shape[0]
    lanes = 2 * hd
    pairs = group // 2
    band = 2 * w
    q = q_ref[...]
    kv = jnp.concatenate([prev_ref[...], cur_ref[...]], axis=0)
    col = lax.broadcasted_iota(jnp.int32, (1, 2 * band), 1) & (band - 1)
    live = (col >= w) | (i > 0) | (col == 0)
    low = lax.broadcasted_iota(jnp.int32, (band, lanes), 1) < hd
    real = lax.broadcasted_iota(jnp.int32, (band, lanes), 0) > 0
    keep_first, keep_second = low & real, (~low) & real
    ones = jnp.concatenate([low, ~low], axis=0).astype(BF16)
    zero = jnp.zeros((), BF16)
    swapped = n_kv * lanes

    def block_diag(slab0, n):
        nat = kv[:, slab0 + (n // 2) * lanes:slab0 + (n // 2 + 1) * lanes]
        swp = kv[:, swapped + slab0 + (n // 2) * lanes:swapped + slab0 + (n // 2 + 1) * lanes]
        first, second = (nat, swp) if n % 2 == 0 else (swp, nat)
        return jnp.concatenate([jnp.where(keep_first, first, zero), jnp.where(keep_second, second, zero)], axis=0)

    for n in range(n_kv):
        k2 = block_diag(0, n)
        v2 = jnp.concatenate([block_diag(n_kv * hd, n), ones], axis=1)
        qn = jnp.concatenate([q[:, (n * pairs + p) * lanes:(n * pairs + p + 1) * lanes] for p in range(pairs)], axis=0)
        s = jnp.where(live, _dot_nt(qn, k2) + bias_ref[n], NEG_INF)
        probs = []
        for half in range(2):
            sh = s[:, half * band:(half + 1) * band]
            probs.append(jnp.exp(sh - jnp.max(sh, axis=1, keepdims=True)).astype(BF16))
        pv = _dot(jnp.concatenate(probs, axis=1), v2)
        o = pv[:, :lanes] / pv[:, lanes:]
        for p in range(pairs):
            o_ref[:, (n * pairs + p) * lanes:(n * pairs + p + 1) * lanes] = o[p * w:(p + 1) * w].astype(BF16)


def _swa_prompt_attn(q, kvb, bias, *, batch, seq, n_kv, group, hd):
    w = WINDOW
    nb = seq // w
    assert group % 2 == 0 and kvb.shape[1] == 4 * n_kv * hd
    return pl.pallas_call(
        functools.partial(_swa_prompt_kernel, n_kv=n_kv, group=group, hd=hd),
        grid=(batch, nb),
        in_specs=[pl.BlockSpec((w, q.shape[1]), lambda b, i: (b * nb + i, 0)),
                  pl.BlockSpec((w, kvb.shape[1]), lambda b, i: (b * nb + jnp.maximum(i - 1, 0), 0)),
                  pl.BlockSpec((w, kvb.shape[1]), lambda b, i: (b * nb + i, 0)),
                  _resident(bias.shape)],
        out_specs=pl.BlockSpec((w, q.shape[1]), lambda b, i: (b * nb + i, 0)),
        out_shape=jax.ShapeDtypeStruct(q.shape, BF16),
        compiler_params=_params("arbitrary", "arbitrary"),
        name="swa_prompt_attn",
    )(q, kvb, kvb, bias)


def _swa_sample_kernel(q_ref, state_ref, new_ref, bias_ref, sink_ref, o_ref, win_ref, *, n_kv, hd, dec_seq):
    n_win = state_ref.shape[2]
    lane = lax.broadcasted_iota(jnp.int32, (1, n_win), 1)
    for sb in range(q_ref.shape[0]):
        state_t = state_ref[sb]
        new = new_ref[sb]
        lead = jnp.zeros((n_win - new.shape[0], new.shape[1]), F32)
        new_t = jnp.concatenate([lead, new], axis=0).T
        win_ref[sb] = jnp.where(lane < n_win - dec_seq, pltpu.roll(state_t, n_win - dec_seq, axis=1), new_t)
        kv_t = jnp.concatenate([state_t, new_t], axis=1).astype(BF16)
        for n in range(n_kv):
            kt = kv_t[n * hd:(n + 1) * hd]
            vt = kv_t[(n_kv + n) * hd:(n_kv + n + 1) * hd]
            s = _dot(q_ref[sb, n], kt) + bias_ref[n]
            sink = sink_ref[n]
            m = jnp.maximum(jnp.max(s, axis=1, keepdims=True), sink)
            p = jnp.exp(s - m)
            denom = jnp.sum(p, axis=1, keepdims=True) + jnp.exp(sink - m)
            o_ref[sb, n] = (_dot_nt(p.astype(BF16), vt) / denom).astype(BF16)


def _swa_sample_attn(q4, state_t, new_pad, bias, sink_rows, *, dec_seq):
    n_seq, n_kv, rows, hd = q4.shape
    width, n_win = state_t.shape[1], state_t.shape[2]
    sb = SWA_SEQS_PER_STEP
    assert n_seq % sb == 0 and bias.shape[2] == 2 * n_win
    blk = lambda i: (i, 0, 0)
    return pl.pallas_call(
        functools.partial(_swa_sample_kernel, n_kv=n_kv, hd=hd, dec_seq=dec_seq),
        grid=(n_seq // sb,),
        in_specs=[pl.BlockSpec((sb, n_kv, rows, hd), lambda i: (i, 0, 0, 0)),
                  pl.BlockSpec((sb, width, n_win), blk), pl.BlockSpec((sb,) + new_pad.shape[1:], blk),
                  _resident(bias.shape), _resident(sink_rows.shape)],
        out_specs=[pl.BlockSpec((sb, n_kv, rows, hd), lambda i: (i, 0, 0, 0)), pl.BlockSpec((sb, width, n_win), blk)],
        out_shape=[jax.ShapeDtypeStruct(q4.shape, BF16), jax.ShapeDtypeStruct(state_t.shape, F32)],
        compiler_params=_params("arbitrary"),
        name="swa_sample_attn",
    )(q4, state_t, new_pad, bias, sink_rows)


def _proj_residual_kernel(o_ref, w_ref, b_ref, h_ref, out_ref):
    out_ref[...] = h_ref[...] + (_dot(o_ref[...], w_ref[...]) + b_ref[...])


def _proj_residual(o, w, b, h):
    m, d = h.shape
    tm = min(ROW_TILE, m)
    row = lambda i: (i, 0)
    return pl.pallas_call(
        _proj_residual_kernel,
        grid=(m // tm,),
        in_specs=[pl.BlockSpec((tm, o.shape[1]), row), _resident(w.shape), _resident((1, d)), pl.BlockSpec((tm, d), row)],
        out_specs=pl.BlockSpec((tm, d), row),
        out_shape=jax.ShapeDtypeStruct((m, d), F32),
        compiler_params=_params("arbitrary"),
        name="swa_out",
    )(o, w, b, h)


def _rope_tables(pos, rope):
    inv_freq = ROPE_THETA ** (-jnp.arange(0, rope, 2, dtype=F32) / rope)
    ang = pos.astype(F32)[:, None] * inv_freq[None, :]
    cos, sin = jnp.cos(ang), jnp.sin(ang)
    return jnp.concatenate([cos, cos], axis=1), jnp.concatenate([sin, sin], axis=1)


def _rot_half_cols(w):
    half = w.shape[-1] // 2
    return jnp.concatenate([-w[..., half:], w[..., :half]], axis=-1)


def _bucket(dist):
    n = jnp.maximum(dist, 0)
    max_exact = N_BUCKETS // 2
    nf = jnp.maximum(n, 1).astype(F32)
    large = max_exact + (jnp.log(nf / max_exact) / math.log(MAX_DISTANCE / max_exact)
                         * (N_BUCKETS - max_exact)).astype(jnp.int32)
    return jnp.where(n < max_exact, n, jnp.minimum(large, N_BUCKETS - 1))


def _window_bias(rel_bias, dist):
    valid = (dist >= 0) & (dist < WINDOW)
    bias = jnp.moveaxis(jnp.take(rel_bias.astype(F32), _bucket(dist), axis=0), -1, 0)
    return jnp.where(valid[None], bias, NEG_INF)


def _band_bias(rel_bias):
    w = WINDOW
    per_dist = _window_bias(rel_bias, jnp.arange(w, dtype=jnp.int32)[None, :])[:, 0, :]
    heads = per_dist.shape[0]
    neg = lambda n: jnp.full((heads, n), NEG_INF, F32)
    line = jnp.concatenate([neg(w), per_dist[:, ::-1], neg(w)], axis=1)
    skew = jnp.tile(line, (1, w))[:, :w * (3 * w - 1)].reshape(heads, w, 3 * w - 1)
    return skew[:, :, w - 1:3 * w - 1]


def kernel(x_prompt, x_sample, cache_mla, state_kv_win, page_table, attn_norm, ffn_norm, final_norm, mla_wq_a, mla_q_norm, mla_wq_b, mla_wkv_a, mla_kv_norm, mla_wkv_b, mla_wo, kv_norm_shared, w_k_shared, b_k_shared, w_v_shared, b_v_shared, swa_wq, swa_bq, swa_sinks, swa_wo, swa_bo, rel_bias, ffn_w_gate, ffn_w_up, ffn_w_down):
    batch, seq, d = x_prompt.shape
    dec_batch, dec_seq, _ = x_sample.shape
    depth = attn_norm.shape[0]
    n_a = mla_wq_a.shape[0]
    past_len = page_table.shape[1] * PAGE_SIZE
    n_win = state_kv_win.shape[1]
    q_lora = mla_wq_a.shape[2]
    kv_lora = mla_kv_norm.shape[1]
    lat = mla_wkv_a.shape[2]
    rope = lat - kv_lora
    heads_a = mla_wq_b.shape[2]
    nope = mla_wq_b.shape[3] - rope
    heads_b, hd = swa_wq.shape[2], swa_wq.shape[3]
    n_kv = w_k_shared.shape[1]
    group = heads_b // n_kv
    assert seq % WINDOW == 0 and n_win == WINDOW and dec_seq <= 8
    mla_scale = 1.0 / math.sqrt(nope + rope)
    swa_scale = 1.0 / math.sqrt(hd)

    vec = lambda a: a.reshape(1, -1).astype(F32)
    streams = [x_prompt.reshape(batch * seq, d), x_sample.reshape(dec_batch * dec_seq, d)]
    rope_tabs = [_rope_tables(jnp.arange(seq, dtype=jnp.int32), rope),
                 _rope_tables(jnp.tile(past_len + jnp.arange(dec_seq, dtype=jnp.int32), dec_batch), rope)]

    cache_t = jnp.swapaxes(cache_mla, 2, 3)
    ffn_wg, ffn_wu, ffn_wd = ffn_w_gate.astype(BF16), ffn_w_up.astype(BF16), ffn_w_down.astype(BF16)

    rows_p_all, rows_s_all = [], []
    kv_p = kv_win_sample = None
    for layer in range(depth):
        ga = vec(attn_norm[layer])
        if layer < n_a:
            a = layer
            wkv_a = mla_wkv_a[a]
            w_in = jnp.concatenate([mla_wq_a[a], wkv_a, _rot_half_cols(wkv_a[:, kv_lora:])], axis=1).astype(BF16)
            wq_pe = mla_wq_b[a][:, :, nope:]
            wqb = jnp.concatenate([mla_wq_b[a][:, :, :nope].reshape(q_lora, heads_a * nope),
                                   wq_pe.reshape(q_lora, heads_a * rope),
                                   _rot_half_cols(wq_pe).reshape(q_lora, heads_a * rope)], axis=1).astype(BF16)
            wkt = jnp.transpose(mla_wkv_b[a][:, :, :nope], (1, 2, 0)).astype(BF16)
            wv = jnp.transpose(mla_wkv_b[a][:, :, nope:], (1, 0, 2)).astype(BF16)
            wo = mla_wo[a].reshape(-1, d).astype(BF16)

            stage = []
            for h, (cos2, sin2) in zip(streams, rope_tabs):
                cq, rows, rows_b = _mla_in(h, ga, w_in, vec(mla_q_norm[a]), vec(mla_kv_norm[a]), cos2, sin2,
                                           q_lora=q_lora, kv_lora=kv_lora, rope=rope)
                stage.append((_mla_q(cq, wqb, wkt, cos2, sin2, scale=mla_scale), rows, rows_b))
            (q_p, rows_p, rows_pb), (q_s, rows_s, rows_sb) = stage
            rows_p_all.append(rows_p.reshape(batch, seq, lat))
            rows_s_all.append(rows_s.reshape(dec_batch, dec_seq, lat))

            o_p = _mla_prompt_attn(q_p, rows_pb, batch=batch, seq=seq, kv_lora=kv_lora)
            q_s = q_s.reshape(heads_a, dec_batch, dec_seq, lat).transpose(1, 0, 2, 3).reshape(dec_batch, heads_a * dec_seq, lat)
            new_pad = jnp.pad(rows_sb.reshape(dec_batch, dec_seq, lat), ((0, 0), (0, NEW_ROWS_PAD - dec_seq), (0, 0)))
            o_s = _mla_sample_attn(q_s, new_pad, cache_t, a, page_table, dec_seq=dec_seq, kv_lora=kv_lora)
            o_s = o_s.reshape(dec_batch, heads_a, dec_seq, kv_lora).transpose(1, 0, 2, 3).reshape(heads_a, dec_batch * dec_seq, kv_lora)
            streams = [_mla_out(o, wv, wo, h) for o, h in zip((o_p, o_s), streams)]
        else:
            b = layer - n_a
            wq = swa_wq[b].reshape(d, heads_b * hd).astype(BF16)
            swap_pairs = lambda t: t.reshape(t.shape[0], n_kv // 2, 2, hd)[:, :, ::-1].reshape(t.shape[0], n_kv * hd)
            wk, wv_s = w_k_shared.reshape(d, n_kv * hd), w_v_shared.reshape(d, n_kv * hd)
            bk, bv = b_k_shared.reshape(1, -1), b_v_shared.reshape(1, -1)
            wkv = jnp.concatenate([wk, wv_s, swap_pairs(wk), swap_pairs(wv_s)], axis=1).astype(BF16)
            bkv = jnp.concatenate([bk, bv, swap_pairs(bk), swap_pairs(bv)], axis=1).astype(F32)
            (q_p, kv_p_new, kvb_p), (q_s, kv_s_new, _) = [
                _swa_in(h, ga, vec(kv_norm_shared), wq, vec(swa_bq[b]), wkv, bkv, scale=swa_scale) for h in streams]
            if b == 0:
                kv_p, kv_pb, kv_s = kv_p_new, kvb_p, kv_s_new
                state = jnp.transpose(state_kv_win, (0, 2, 3, 4, 1)).reshape(dec_batch, 2 * n_kv * hd, n_win)
            sinks = swa_sinks[b].astype(F32)

            bias_p = _band_bias(rel_bias).at[:, :, 0].set(sinks[:, None])
            pairs = group // 2
            bias_p = bias_p.reshape(n_kv, pairs, 2, WINDOW, 2 * WINDOW)
            bias_p = bias_p.transpose(0, 1, 3, 2, 4).reshape(n_kv, pairs * WINDOW, 4 * WINDOW)
            o_p = _swa_prompt_attn(q_p, kv_pb, bias_p, batch=batch, seq=seq, n_kv=n_kv, group=group, hd=hd)

            n_keys = 2 * n_win
            t_idx = jnp.arange(dec_seq, dtype=jnp.int32)[:, None]
            j_idx = jnp.arange(n_keys, dtype=jnp.int32)[None, :]
            c_idx = jnp.where(j_idx < n_win, j_idx, j_idx - (n_win - dec_seq))
            real = (j_idx < n_win) | (j_idx >= n_keys - dec_seq)
            dist_s = jnp.where(real, n_win + t_idx - c_idx, -1)
            bias_s = _window_bias(rel_bias, dist_s)
            bias_s = bias_s.reshape(n_kv, group * dec_seq, n_keys)
            sink_s = jnp.repeat(sinks, dec_seq).reshape(n_kv, group * dec_seq, 1)
            q4 = q_s.reshape(dec_batch, dec_seq, n_kv, group, hd).transpose(0, 2, 3, 1, 4).reshape(dec_batch, n_kv, group * dec_seq, hd)
            new_pad = jnp.pad(kv_s.reshape(dec_batch, dec_seq, -1), ((0, 0), (8 - dec_seq, 0), (0, 0)))
            o4, win = _swa_sample_attn(q4, state, new_pad, bias_s, sink_s, dec_seq=dec_seq)
            if b == 0:
                kv_win_sample = win
            o_s = o4.reshape(dec_batch, n_kv, group, dec_seq, hd).transpose(0, 3, 1, 2, 4).reshape(dec_batch * dec_seq, heads_b * hd)
            wo = swa_wo[b].reshape(heads_b * hd, d).astype(BF16)
            streams = [_proj_residual(o, wo, vec(swa_bo[b]), h) for o, h in zip((o_p, o_s), streams)]

        last = layer == depth - 1
        streams = [_ffn(h, vec(ffn_norm[layer]), ffn_wg, ffn_wu, ffn_wd, vec(final_norm), layer, final_norm=last)
                   for h in streams]

    y_prompt = streams[0].reshape(batch, seq, d)
    y_sample = streams[1].reshape(dec_batch, dec_seq, d)
    n_tail = min(WINDOW, seq)
    kv_win_prompt = kv_p.reshape(batch, seq, -1)[:, seq - n_tail:].reshape(batch, n_tail, 2, n_kv, hd)
    kv_win_sample = jnp.transpose(kv_win_sample.reshape(dec_batch, 2, n_kv, hd, n_win), (0, 4, 1, 2, 3))
    return (y_prompt, y_sample, jnp.stack(rows_p_all, axis=0), jnp.stack(rows_s_all, axis=0),
            kv_win_prompt, kv_win_sample)
```

```python
import functools
import math

import jax
import jax.numpy as jnp
from jax import lax
from jax.experimental import pallas as pl
from jax.experimental.pallas import tpu as pltpu

F32 = jnp.float32
BF16 = jnp.bfloat16

EPS = 1e-6
NEG_INF = -1e30
ROPE_THETA = 10000.0
PAGE_SIZE = 128
WINDOW = 128
N_BUCKETS = 32
MAX_DISTANCE = 128

VMEM_LIMIT_BYTES = 52 * 1024 * 1024
ROW_TILE = 512
FFN_TILE = 512
FFN_CHUNKS = 2
MLA_Q_TILE = 256
MLA_K_TILE = 512
MLA_HEADS_PER_PASS = 2
PAGES_PER_STEP = 32
SAMPLE_CHAINS = 2
PAGE_RING_STEPS = 2
DMA_PRIORITIES = 2
NEW_ROWS_PAD = 16
SWA_SEQS_PER_STEP = 8


def _params(*semantics):
    return pltpu.CompilerParams(dimension_semantics=semantics, vmem_limit_bytes=VMEM_LIMIT_BYTES)


def _rms(x, g):
    return x * lax.rsqrt(jnp.mean(x * x, axis=-1, keepdims=True) + EPS) * g


def _dot(a, b):
    return jnp.dot(a, b, preferred_element_type=F32)


def _dot_nt(a, b):
    return lax.dot_general(a, b, (((1,), (1,)), ((), ())), preferred_element_type=F32)


def _resident(shape):
    return pl.BlockSpec(shape, lambda *_: (0,) * len(shape))


def _table_index(table_rows, tm):
    assert table_rows % tm == 0
    blocks = table_rows // tm
    return lambda i: (i % blocks, 0)


def _mla_in_kernel(h_ref, g_ref, w_ref, qn_ref, kvn_ref, cos_ref, sin_ref,
                   cq_ref, rows_ref, rowsb_ref, *, q_lora, kv_lora, rope):
    xn = _rms(h_ref[...], g_ref[...]).astype(BF16)
    y = _dot(xn, w_ref[...])
    cq_ref[...] = _rms(y[:, :q_lora], qn_ref[...]).astype(BF16)
    c0 = q_lora
    ckv = _rms(y[:, c0:c0 + kv_lora], kvn_ref[...])
    p0 = c0 + kv_lora
    kpe = y[:, p0:p0 + rope] * cos_ref[...] + y[:, p0 + rope:p0 + 2 * rope] * sin_ref[...]
    rows_ref[:, :kv_lora] = ckv
    rows_ref[:, kv_lora:] = kpe
    rowsb_ref[:, :kv_lora] = ckv.astype(BF16)
    rowsb_ref[:, kv_lora:] = kpe.astype(BF16)


def _mla_in(h, g, w, qn, kvn, cos2, sin2, *, q_lora, kv_lora, rope):
    m, d = h.shape
    tm = min(ROW_TILE, m)
    lat = kv_lora + rope
    row = lambda i: (i, 0)
    tab = _table_index(cos2.shape[0], tm)
    return pl.pallas_call(
        functools.partial(_mla_in_kernel, q_lora=q_lora, kv_lora=kv_lora, rope=rope),
        grid=(m // tm,),
        in_specs=[pl.BlockSpec((tm, d), row), _resident((1, d)), _resident(w.shape),
                  _resident((1, q_lora)), _resident((1, kv_lora)),
                  pl.BlockSpec((tm, rope), tab), pl.BlockSpec((tm, rope), tab)],
        out_specs=[pl.BlockSpec((tm, q_lora), row), pl.BlockSpec((tm, lat), row), pl.BlockSpec((tm, lat), row)],
        out_shape=[jax.ShapeDtypeStruct((m, q_lora), BF16), jax.ShapeDtypeStruct((m, lat), F32),
                   jax.ShapeDtypeStruct((m, lat), BF16)],
        compiler_params=_params("arbitrary"),
        name="mla_in",
    )(h, g, w, qn, kvn, cos2, sin2)


def _mla_q_kernel(cq_ref, wqb_ref, wkt_ref, cos_ref, sin_ref, q_ref, *, heads, nope, rope, kv_lora, scale):
    y = _dot(cq_ref[...], wqb_ref[...])
    cos = cos_ref[...]
    sin = sin_ref[...]
    pe0 = heads * nope
    rot0 = pe0 + heads * rope
    for h in range(heads):
        qn = y[:, h * nope:(h + 1) * nope].astype(BF16)
        q_ref[h, :, :kv_lora] = (_dot(qn, wkt_ref[h]) * scale).astype(BF16)
        pe = y[:, pe0 + h * rope:pe0 + (h + 1) * rope] * cos + y[:, rot0 + h * rope:rot0 + (h + 1) * rope] * sin
        q_ref[h, :, kv_lora:] = (pe * scale).astype(BF16)


def _mla_q(cq, wqb, wkt, cos2, sin2, *, scale):
    m, q_lora = cq.shape
    heads, nope, kv_lora = wkt.shape
    rope = cos2.shape[1]
    tm = min(256, m)
    row = lambda i: (i, 0)
    tab = _table_index(cos2.shape[0], tm)
    return pl.pallas_call(
        functools.partial(_mla_q_kernel, heads=heads, nope=nope, rope=rope, kv_lora=kv_lora, scale=scale),
        grid=(m // tm,),
        in_specs=[pl.BlockSpec((tm, q_lora), row), _resident(wqb.shape), _resident(wkt.shape),
                  pl.BlockSpec((tm, rope), tab), pl.BlockSpec((tm, rope), tab)],
        out_specs=pl.BlockSpec((heads, tm, kv_lora + rope), lambda i: (0, i, 0)),
        out_shape=jax.ShapeDtypeStruct((heads, m, kv_lora + rope), BF16),
        compiler_params=_params("arbitrary"),
        name="mla_q",
    )(cq, wqb, wkt, cos2, sin2)


def _softmax_step(sc, pv, m_sc, l_sc, acc_sc, r=slice(None)):
    m_prev = m_sc[r]
    m_new = jnp.maximum(m_prev, jnp.max(sc, axis=1, keepdims=True))
    alpha = jnp.exp(m_prev - m_new)
    p = jnp.exp(sc - m_new)
    l_sc[r] = alpha * l_sc[r] + jnp.sum(p, axis=1, keepdims=True)
    acc_sc[r] = alpha * acc_sc[r] + pv(p.astype(BF16))
    m_sc[r] = m_new


def _mla_prompt_kernel(qi_ref, kj_ref, q_ref, k_ref, o_ref, m_sc, l_sc, acc_sc, *, heads, tq, tk, kv_lora):
    s = pl.program_id(1)
    qi = qi_ref[s]
    kj = kj_ref[s]
    diag = (qi * tq + tq - 1) // tk
    hp = MLA_HEADS_PER_PASS
    rows = hp * tq

    @pl.when(kj == 0)
    def _():
        m_sc[...] = jnp.full_like(m_sc, NEG_INF)
        l_sc[...] = jnp.zeros_like(l_sc)
        acc_sc[...] = jnp.zeros_like(acc_sc)

    def step(mask):
        k = k_ref[...]
        v = k[:, :kv_lora]

        def scores(g):
            sc = _dot_nt(q_ref[g * hp:(g + 1) * hp].reshape(rows, q_ref.shape[-1]), k)
            return sc if mask is None else jnp.where(mask, sc, NEG_INF)

        n_groups = heads // hp
        sc = scores(0)
        for g in range(n_groups):
            nxt = scores(g + 1) if g + 1 < n_groups else None
            _softmax_step(sc, lambda p: _dot(p, v), m_sc, l_sc, acc_sc, slice(g * rows, (g + 1) * rows))
            sc = nxt

    @pl.when(kj != diag)
    def _():
        step(None)

    @pl.when(kj == diag)
    def _():
        q_pos = qi * tq + (lax.broadcasted_iota(jnp.int32, (rows, tk), 0) & (tq - 1))
        k_pos = kj * tk + lax.broadcasted_iota(jnp.int32, (rows, tk), 1)
        step(k_pos <= q_pos)
        o_ref[...] = (acc_sc[...] / l_sc[...]).astype(BF16).reshape(o_ref.shape)


def _mla_prompt_attn(q_full, rows_b, *, batch, seq, kv_lora):
    heads, _, lat = q_full.shape
    tq, tk = MLA_Q_TILE, MLA_K_TILE
    assert tq & (tq - 1) == 0 and seq % tq == 0 and seq % tk == 0 and tk % tq == 0
    nq, nk = seq // tq, seq // tk
    qi_list, kj_list = [], []
    for qi in range(nq):
        for kj in range((qi * tq + tq - 1) // tk + 1):
            qi_list.append(qi)
            kj_list.append(kj)
    qi_tab = jnp.asarray(qi_list, jnp.int32)
    kj_tab = jnp.asarray(kj_list, jnp.int32)
    rows = heads * tq
    grid_spec = pltpu.PrefetchScalarGridSpec(
        num_scalar_prefetch=2,
        grid=(batch, len(qi_list)),
        in_specs=[pl.BlockSpec((heads, tq, lat), lambda b, s, qi, kj: (0, b * nq + qi[s], 0)),
                  pl.BlockSpec((tk, lat), lambda b, s, qi, kj: (b * nk + kj[s], 0))],
        out_specs=pl.BlockSpec((heads, tq, kv_lora), lambda b, s, qi, kj: (0, b * nq + qi[s], 0)),
        scratch_shapes=[pltpu.VMEM((rows, 1), F32), pltpu.VMEM((rows, 1), F32), pltpu.VMEM((rows, kv_lora), F32)],
    )
    return pl.pallas_call(
        functools.partial(_mla_prompt_kernel, heads=heads, tq=tq, tk=tk, kv_lora=kv_lora),
        grid_spec=grid_spec,
        out_shape=jax.ShapeDtypeStruct((heads, batch * seq, kv_lora), BF16),
        compiler_params=_params("arbitrary", "arbitrary"),
        name="mla_prompt_attn",
    )(qi_tab, kj_tab, q_full, rows_b)


def _mla_sample_kernel(pt_ref, q_ref, new_ref, cache_ref, o_ref, ring, sems, kt_sc, m_sc, l_sc, acc_sc, *,
                       layer, n_pages, dec_seq, kv_lora):
    s, g = pl.program_id(0), pl.program_id(1)
    steps = pl.num_programs(1)
    ring_size = ring.shape[0]
    per_seq = pt_ref.shape[1]
    total = pt_ref.shape[0] * per_seq
    base = (s * steps + g) * n_pages
    slot0 = (g % (ring_size // n_pages)) * n_pages

    def page_copy(n, slot):
        page = pt_ref[lax.shift_right_logical(n, per_seq.bit_length() - 1), n & (per_seq - 1)]
        return pltpu.make_async_copy(cache_ref.at[layer, page], ring.at[slot], sems.at[slot])

    @pl.when((s == 0) & (g == 0))
    def _():
        for j in range(ring_size):
            page_copy(j, j).start(priority=j % DMA_PRIORITIES)

    @pl.when(g == 0)
    def _():
        m_sc[...] = jnp.full_like(m_sc, NEG_INF)
        l_sc[...] = jnp.zeros_like(l_sc)
        acc_sc[...] = jnp.zeros_like(acc_sc)

    chains = m_sc.shape[0]
    per = n_pages // chains
    width = ring.shape[2]
    for j in range(n_pages):
        page_copy(base + j, slot0 + j).wait()
        kt_sc[j // per, :, (j % per) * width:(j % per + 1) * width] = ring[slot0 + j].astype(BF16)

    @pl.when(base + ring_size < total)
    def _():
        for j in range(n_pages):
            page_copy(base + ring_size + j, slot0 + j).start(priority=j % DMA_PRIORITIES)

    q = q_ref[...]
    scs = [_dot(q, kt_sc[c]) for c in range(chains)]
    for c in range(chains):
        _softmax_step(scs[c], lambda p, c=c: _dot_nt(p, kt_sc[c, :kv_lora]), m_sc, l_sc, acc_sc, c)

    @pl.when(g == pl.num_programs(1) - 1)
    def _():
        kn = new_ref[...]
        sc = _dot_nt(q, kn)
        t = lax.broadcasted_iota(jnp.int32, sc.shape, 0) % dec_seq
        c = lax.broadcasted_iota(jnp.int32, sc.shape, 1)
        _softmax_step(jnp.where(c <= t, sc, NEG_INF), lambda p: _dot(p, kn[:, :kv_lora]), m_sc, l_sc, acc_sc, 0)
        m = functools.reduce(jnp.maximum, [m_sc[c] for c in range(chains)])
        weights = [jnp.exp(m_sc[c] - m) for c in range(chains)]
        l = sum(l_sc[c] * weights[c] for c in range(chains))
        acc = sum(acc_sc[c] * weights[c] for c in range(chains))
        o_ref[...] = (acc / l).astype(BF16)


def _mla_sample_attn(q_s, new_pad, cache_t, layer, page_table, *, dec_seq, kv_lora):
    n_seq, rows, lat = q_s.shape
    per_seq = page_table.shape[1]
    p = PAGES_PER_STEP
    ring_size = PAGE_RING_STEPS * p
    assert per_seq & (per_seq - 1) == 0 and per_seq % ring_size == 0 and p % SAMPLE_CHAINS == 0
    assert cache_t.shape[2:] == (lat, PAGE_SIZE)
    grid_spec = pltpu.PrefetchScalarGridSpec(
        num_scalar_prefetch=1,
        grid=(n_seq, per_seq // p),
        in_specs=[pl.BlockSpec((None, rows, lat), lambda s, g, pt: (s, 0, 0)),
                  pl.BlockSpec((None, NEW_ROWS_PAD, lat), lambda s, g, pt: (s, 0, 0)),
                  pl.BlockSpec(memory_space=pl.ANY)],
        out_specs=pl.BlockSpec((None, rows, kv_lora), lambda s, g, pt: (s, 0, 0)),
        scratch_shapes=[pltpu.VMEM((ring_size, lat, PAGE_SIZE), F32), pltpu.SemaphoreType.DMA((ring_size,)),
                        pltpu.VMEM((SAMPLE_CHAINS, lat, p // SAMPLE_CHAINS * PAGE_SIZE), BF16),
                        pltpu.VMEM((SAMPLE_CHAINS, rows, 1), F32), pltpu.VMEM((SAMPLE_CHAINS, rows, 1), F32),
                        pltpu.VMEM((SAMPLE_CHAINS, rows, kv_lora), F32)],
    )
    return pl.pallas_call(
        functools.partial(_mla_sample_kernel, layer=layer, n_pages=p, dec_seq=dec_seq, kv_lora=kv_lora),
        grid_spec=grid_spec,
        out_shape=jax.ShapeDtypeStruct((n_seq, rows, kv_lora), BF16),
        compiler_params=_params("arbitrary", "arbitrary"),
        name="mla_sample_attn",
    )(page_table, q_s, new_pad, cache_t)


def _mla_out_kernel(o_ref, wv_ref, wo_ref, h_ref, out_ref, *, heads):
    v = jnp.concatenate([_dot(o_ref[h], wv_ref[h]).astype(BF16) for h in range(heads)], axis=1)
    out_ref[...] = h_ref[...] + _dot(v, wo_ref[...])


def _mla_out(o_lat, wv, wo, h):
    heads, m, kv_lora = o_lat.shape
    d = h.shape[1]
    tm = min(256, m)
    row = lambda i: (i, 0)
    return pl.pallas_call(
        functools.partial(_mla_out_kernel, heads=heads),
        grid=(m // tm,),
        in_specs=[pl.BlockSpec((heads, tm, kv_lora), lambda i: (0, i, 0)), _resident(wv.shape), _resident(wo.shape),
                  pl.BlockSpec((tm, d), row)],
        out_specs=pl.BlockSpec((tm, d), row),
        out_shape=jax.ShapeDtypeStruct((m, d), F32),
        compiler_params=_params("arbitrary"),
        name="mla_out",
    )(o_lat, wv, wo, h)


def _ffn_kernel(h_ref, g_ref, wg_ref, wu_ref, wd_ref, fg_ref, out_ref, xn_sc, acc_sc, *, final_norm):
    j = pl.program_id(1)

    @pl.when(j == 0)
    def _():
        xn_sc[...] = _rms(h_ref[...], g_ref[...]).astype(BF16)
        acc_sc[...] = jnp.zeros_like(acc_sc)

    xn = xn_sc[...]
    chunk = wg_ref.shape[1] // FFN_CHUNKS

    def gate_up(c):
        cols = slice(c * chunk, (c + 1) * chunk)
        return _dot(xn, wg_ref[:, cols]), _dot(xn, wu_ref[:, cols])

    cur = gate_up(0)
    total = None
    for c in range(FFN_CHUNKS):
        nxt = gate_up(c + 1) if c + 1 < FFN_CHUNKS else None
        a, u = cur
        part = _dot((a * jax.nn.sigmoid(a) * u).astype(BF16), wd_ref[c * chunk:(c + 1) * chunk, :])
        total = part if total is None else total + part
        cur = nxt
    acc_sc[...] += total

    @pl.when(j == pl.num_programs(1) - 1)
    def _():
        res = h_ref[...] + acc_sc[...]
        out_ref[...] = _rms(res, fg_ref[...]) if final_norm else res


def _ffn(h, g, wg, wu, wd, fg, layer, *, final_norm):
    m, d = h.shape
    dff = wg.shape[2]
    tm = min(ROW_TILE, m)
    tf = FFN_TILE
    assert dff % tf == 0
    row = lambda i, j: (i, 0)
    return pl.pallas_call(
        functools.partial(_ffn_kernel, final_norm=final_norm),
        grid=(m // tm, dff // tf),
        in_specs=[pl.BlockSpec((tm, d), row), _resident((1, d)),
                  pl.BlockSpec((None, d, tf), lambda i, j: (layer, 0, j)),
                  pl.BlockSpec((None, d, tf), lambda i, j: (layer, 0, j)),
                  pl.BlockSpec((None, tf, d), lambda i, j: (layer, j, 0)), _resident((1, d))],
        out_specs=pl.BlockSpec((tm, d), row),
        out_shape=jax.ShapeDtypeStruct((m, d), F32),
        scratch_shapes=[pltpu.VMEM((tm, d), BF16), pltpu.VMEM((tm, d), F32)],
        compiler_params=_params("arbitrary", "arbitrary"),
        name="ffn",
    )(h, g, wg, wu, wd, fg)


def _swa_in_kernel(h_ref, ga_ref, gk_ref, wq_ref, bq_ref, wkv_ref, bkv_ref, q_ref, kv_ref, kvb_ref, *, scale):
    x = h_ref[...]
    y = x * lax.rsqrt(jnp.mean(x * x, axis=-1, keepdims=True) + EPS)
    xn = (y * ga_ref[...]).astype(BF16)
    hn = (y * gk_ref[...]).astype(BF16)
    q_ref[...] = ((_dot(xn, wq_ref[...]) + bq_ref[...]) * scale).astype(BF16)
    kv = _dot(hn, wkv_ref[...]) + bkv_ref[...]
    kv_ref[...] = kv[:, :kv_ref.shape[1]]
    kvb_ref[...] = kv.astype(BF16)


def _swa_in(h, ga, gk, wq, bq, wkv, bkv, *, scale):
    m, d = h.shape
    nq, nkv = wq.shape[1], wkv.shape[1]
    tm = min(256, m)
    row = lambda i: (i, 0)
    return pl.pallas_call(
        functools.partial(_swa_in_kernel, scale=scale),
        grid=(m // tm,),
        in_specs=[pl.BlockSpec((tm, d), row), _resident((1, d)), _resident((1, d)), _resident(wq.shape),
                  _resident((1, nq)), _resident(wkv.shape), _resident((1, nkv))],
        out_specs=[pl.BlockSpec((tm, nq), row), pl.BlockSpec((tm, nkv // 2), row), pl.BlockSpec((tm, nkv), row)],
        out_shape=[jax.ShapeDtypeStruct((m, nq), BF16), jax.ShapeDtypeStruct((m, nkv // 2), F32),
                   jax.ShapeDtypeStruct((m, nkv), BF16)],
        compiler_params=_params("arbitrary"),
        name="swa_in",
    )(h, ga, gk, wq, bq, wkv, bkv)


def _swa_prompt_kernel(q_ref, prev_ref, cur_ref, bias_ref, o_ref, *, n_kv, group, hd):
    i = pl.program_id(1)
    w = q_ref.shape[0]
    lanes = 2 * hd
    pairs = group // 2
    band = 2 * w
    q = q_ref[...]
    kv = jnp.concatenate([prev_ref[...], cur_ref[...]], axis=0)
    col = lax.broadcasted_iota(jnp.int32, (1, 2 * band), 1) & (band - 1)
    live = (col >= w) | (i > 0) | (col == 0)
    low = lax.broadcasted_iota(jnp.int32, (band, lanes), 1) < hd
    real = lax.broadcasted_iota(jnp.int32, (band, lanes), 0) > 0
    keep_first, keep_second = low & real, (~low) & real
    ones = jnp.concatenate([low, ~low], axis=0).astype(BF16)
    zero = jnp.zeros((), BF16)
    swapped = n_kv * lanes

    def block_diag(slab0, n):
        nat = kv[:, slab0 + (n // 2) * lanes:slab0 + (n // 2 + 1) * lanes]
        swp = kv[:, swapped + slab0 + (n // 2) * lanes:swapped + slab0 + (n // 2 + 1) * lanes]
        first, second = (nat, swp) if n % 2 == 0 else (swp, nat)
        return jnp.concatenate([jnp.where(keep_first, first, zero), jnp.where(keep_second, second, zero)], axis=0)

    for n in range(n_kv):
        k2 = block_diag(0, n)
        v2 = jnp.concatenate([block_diag(n_kv * hd, n), ones], axis=1)
        qn = jnp.concatenate([q[:, (n * pairs + p) * lanes:(n * pairs + p + 1) * lanes] for p in range(pairs)], axis=0)
        s = jnp.where(live, _dot_nt(qn, k2) + bias_ref[n], NEG_INF)
        probs = []
        for half in range(2):
            sh = s[:, half * band:(half + 1) * band]
            probs.append(jnp.exp(sh - jnp.max(sh, axis=1, keepdims=True)).astype(BF16))
        pv = _dot(jnp.concatenate(probs, axis=1), v2)
        o = pv[:, :lanes] / pv[:, lanes:]
        for p in range(pairs):
            o_ref[:, (n * pairs + p) * lanes:(n * pairs + p + 1) * lanes] = o[p * w:(p + 1) * w].astype(BF16)


def _swa_prompt_attn(q, kvb, bias, *, batch, seq, n_kv, group, hd):
    w = WINDOW
    nb = seq // w
    assert group % 2 == 0 and kvb.shape[1] == 4 * n_kv * hd
    return pl.pallas_call(
        functools.partial(_swa_prompt_kernel, n_kv=n_kv, group=group, hd=hd),
        grid=(batch, nb),
        in_specs=[pl.BlockSpec((w, q.shape[1]), lambda b, i: (b * nb + i, 0)),
                  pl.BlockSpec((w, kvb.shape[1]), lambda b, i: (b * nb + jnp.maximum(i - 1, 0), 0)),
                  pl.BlockSpec((w, kvb.shape[1]), lambda b, i: (b * nb + i, 0)),
                  _resident(bias.shape)],
        out_specs=pl.BlockSpec((w, q.shape[1]), lambda b, i: (b * nb + i, 0)),
        out_shape=jax.ShapeDtypeStruct(q.shape, BF16),
        compiler_params=_params("arbitrary", "arbitrary"),
        name="swa_prompt_attn",
    )(q, kvb, kvb, bias)


def _swa_sample_kernel(q_ref, state_ref, new_ref, bias_ref, sink_ref, o_ref, win_ref, *, n_kv, hd, dec_seq):
    n_win = state_ref.shape[2]
    lane = lax.broadcasted_iota(jnp.int32, (1, n_win), 1)
    for sb in range(q_ref.shape[0]):
        state_t = state_ref[sb]
        new = new_ref[sb]
        lead = jnp.zeros((n_win - new.shape[0], new.shape[1]), F32)
        new_t = jnp.concatenate([lead, new], axis=0).T
        win_ref[sb] = jnp.where(lane < n_win - dec_seq, pltpu.roll(state_t, n_win - dec_seq, axis=1), new_t)
        kv_t = jnp.concatenate([state_t, new_t], axis=1).astype(BF16)
        for n in range(n_kv):
            kt = kv_t[n * hd:(n + 1) * hd]
            vt = kv_t[(n_kv + n) * hd:(n_kv + n + 1) * hd]
            s = _dot(q_ref[sb, n], kt) + bias_ref[n]
            sink = sink_ref[n]
            m = jnp.maximum(jnp.max(s, axis=1, keepdims=True), sink)
            p = jnp.exp(s - m)
            denom = jnp.sum(p, axis=1, keepdims=True) + jnp.exp(sink - m)
            o_ref[sb, n] = (_dot_nt(p.astype(BF16), vt) / denom).astype(BF16)


def _swa_sample_attn(q4, state_t, new_pad, bias, sink_rows, *, dec_seq):
    n_seq, n_kv, rows, hd = q4.shape
    width, n_win = state_t.shape[1], state_t.shape[2]
    sb = SWA_SEQS_PER_STEP
    assert n_seq % sb == 0 and bias.shape[2] == 2 * n_win
    blk = lambda i: (i, 0, 0)
    return pl.pallas_call(
        functools.partial(_swa_sample_kernel, n_kv=n_kv, hd=hd, dec_seq=dec_seq),
        grid=(n_seq // sb,),
        in_specs=[pl.BlockSpec((sb, n_kv, rows, hd), lambda i: (i, 0, 0, 0)),
                  pl.BlockSpec((sb, width, n_win), blk), pl.BlockSpec((sb,) + new_pad.shape[1:], blk),
                  _resident(bias.shape), _resident(sink_rows.shape)],
        out_specs=[pl.BlockSpec((sb, n_kv, rows, hd), lambda i: (i, 0, 0, 0)), pl.BlockSpec((sb, width, n_win), blk)],
        out_shape=[jax.ShapeDtypeStruct(q4.shape, BF16), jax.ShapeDtypeStruct(state_t.shape, F32)],
        compiler_params=_params("arbitrary"),
        name="swa_sample_attn",
    )(q4, state_t, new_pad, bias, sink_rows)


def _proj_residual_kernel(o_ref, w_ref, b_ref, h_ref, out_ref):
    out_ref[...] = h_ref[...] + (_dot(o_ref[...], w_ref[...]) + b_ref[...])


def _proj_residual(o, w, b, h):
    m, d = h.shape
    tm = min(ROW_TILE, m)
    row = lambda i: (i, 0)
    return pl.pallas_call(
        _proj_residual_kernel,
        grid=(m // tm,),
        in_specs=[pl.BlockSpec((tm, o.shape[1]), row), _resident(w.shape), _resident((1, d)), pl.BlockSpec((tm, d), row)],
        out_specs=pl.BlockSpec((tm, d), row),
        out_shape=jax.ShapeDtypeStruct((m, d), F32),
        compiler_params=_params("arbitrary"),
        name="swa_out",
    )(o, w, b, h)


def _rope_tables(pos, rope):
    inv_freq = ROPE_THETA ** (-jnp.arange(0, rope, 2, dtype=F32) / rope)
    ang = pos.astype(F32)[:, None] * inv_freq[None, :]
    cos, sin = jnp.cos(ang), jnp.sin(ang)
    return jnp.concatenate([cos, cos], axis=1), jnp.concatenate([sin, sin], axis=1)


def _rot_half_cols(w):
    half = w.shape[-1] // 2
    return jnp.concatenate([-w[..., half:], w[..., :half]], axis=-1)


def _bucket(dist):
    n = jnp.maximum(dist, 0)
    max_exact = N_BUCKETS // 2
    nf = jnp.maximum(n, 1).astype(F32)
    large = max_exact + (jnp.log(nf / max_exact) / math.log(MAX_DISTANCE / max_exact)
                         * (N_BUCKETS - max_exact)).astype(jnp.int32)
    return jnp.where(n < max_exact, n, jnp.minimum(large, N_BUCKETS - 1))


def _window_bias(rel_bias, dist):
    valid = (dist >= 0) & (dist < WINDOW)
    bias = jnp.moveaxis(jnp.take(rel_bias.astype(F32), _bucket(dist), axis=0), -1, 0)
    return jnp.where(valid[None], bias, NEG_INF)


def _band_bias(rel_bias):
    w = WINDOW
    per_dist = _window_bias(rel_bias, jnp.arange(w, dtype=jnp.int32)[None, :])[:, 0, :]
    heads = per_dist.shape[0]
    neg = lambda n: jnp.full((heads, n), NEG_INF, F32)
    line = jnp.concatenate([neg(w), per_dist[:, ::-1], neg(w)], axis=1)
    skew = jnp.tile(line, (1, w))[:, :w * (3 * w - 1)].reshape(heads, w, 3 * w - 1)
    return skew[:, :, w - 1:3 * w - 1]


def kernel(x_prompt, x_sample, cache_mla, state_kv_win, page_table, attn_norm, ffn_norm, final_norm, mla_wq_a, mla_q_norm, mla_wq_b, mla_wkv_a, mla_kv_norm, mla_wkv_b, mla_wo, kv_norm_shared, w_k_shared, b_k_shared, w_v_shared, b_v_shared, swa_wq, swa_bq, swa_sinks, swa_wo, swa_bo, rel_bias, ffn_w_gate, ffn_w_up, ffn_w_down):
    batch, seq, d = x_prompt.shape
    dec_batch, dec_seq, _ = x_sample.shape
    depth = attn_norm.shape[0]
    n_a = mla_wq_a.shape[0]
    past_len = page_table.shape[1] * PAGE_SIZE
    n_win = state_kv_win.shape[1]
    q_lora = mla_wq_a.shape[2]
    kv_lora = mla_kv_norm.shape[1]
    lat = mla_wkv_a.shape[2]
    rope = lat - kv_lora
    heads_a = mla_wq_b.shape[2]
    nope = mla_wq_b.shape[3] - rope
    heads_b, hd = swa_wq.shape[2], swa_wq.shape[3]
    n_kv = w_k_shared.shape[1]
    group = heads_b // n_kv
    assert seq % WINDOW == 0 and n_win == WINDOW and dec_seq <= 8
    mla_scale = 1.0 / math.sqrt(nope + rope)
    swa_scale = 1.0 / math.sqrt(hd)

    vec = lambda a: a.reshape(1, -1).astype(F32)
    streams = [x_prompt.reshape(batch * seq, d), x_sample.reshape(dec_batch * dec_seq, d)]
    rope_tabs = [_rope_tables(jnp.arange(seq, dtype=jnp.int32), rope),
                 _rope_tables(jnp.tile(past_len + jnp.arange(dec_seq, dtype=jnp.int32), dec_batch), rope)]

    cache_t = jnp.swapaxes(cache_mla, 2, 3)
    ffn_wg, ffn_wu, ffn_wd = ffn_w_gate.astype(BF16), ffn_w_up.astype(BF16), ffn_w_down.astype(BF16)

    rows_p_all, rows_s_all = [], []
    kv_p = kv_win_sample = None
    for layer in range(depth):
        ga = vec(attn_norm[layer])
        if layer < n_a:
            a = layer
            wkv_a = mla_wkv_a[a]
            w_in = jnp.concatenate([mla_wq_a[a], wkv_a, _rot_half_cols(wkv_a[:, kv_lora:])], axis=1).astype(BF16)
            wq_pe = mla_wq_b[a][:, :, nope:]
            wqb = jnp.concatenate([mla_wq_b[a][:, :, :nope].reshape(q_lora, heads_a * nope),
                                   wq_pe.reshape(q_lora, heads_a * rope),
                                   _rot_half_cols(wq_pe).reshape(q_lora, heads_a * rope)], axis=1).astype(BF16)
            wkt = jnp.transpose(mla_wkv_b[a][:, :, :nope], (1, 2, 0)).astype(BF16)
            wv = jnp.transpose(mla_wkv_b[a][:, :, nope:], (1, 0, 2)).astype(BF16)
            wo = mla_wo[a].reshape(-1, d).astype(BF16)

            stage = []
            for h, (cos2, sin2) in zip(streams, rope_tabs):
                cq, rows, rows_b = _mla_in(h, ga, w_in, vec(mla_q_norm[a]), vec(mla_kv_norm[a]), cos2, sin2,
                                           q_lora=q_lora, kv_lora=kv_lora, rope=rope)
                stage.append((_mla_q(cq, wqb, wkt, cos2, sin2, scale=mla_scale), rows, rows_b))
            (q_p, rows_p, rows_pb), (q_s, rows_s, rows_sb) = stage
            rows_p_all.append(rows_p.reshape(batch, seq, lat))
            rows_s_all.append(rows_s.reshape(dec_batch, dec_seq, lat))

            o_p = _mla_prompt_attn(q_p, rows_pb, batch=batch, seq=seq, kv_lora=kv_lora)
            q_s = q_s.reshape(heads_a, dec_batch, dec_seq, lat).transpose(1, 0, 2, 3).reshape(dec_batch, heads_a * dec_seq, lat)
            new_pad = jnp.pad(rows_sb.reshape(dec_batch, dec_seq, lat), ((0, 0), (0, NEW_ROWS_PAD - dec_seq), (0, 0)))
            o_s = _mla_sample_attn(q_s, new_pad, cache_t, a, page_table, dec_seq=dec_seq, kv_lora=kv_lora)
            o_s = o_s.reshape(dec_batch, heads_a, dec_seq, kv_lora).transpose(1, 0, 2, 3).reshape(heads_a, dec_batch * dec_seq, kv_lora)
            streams = [_mla_out(o, wv, wo, h) for o, h in zip((o_p, o_s), streams)]
        else:
            b = layer - n_a
            wq = swa_wq[b].reshape(d, heads_b * hd).astype(BF16)
            swap_pairs = lambda t: t.reshape(t.shape[0], n_kv // 2, 2, hd)[:, :, ::-1].reshape(t.shape[0], n_kv * hd)
            wk, wv_s = w_k_shared.reshape(d, n_kv * hd), w_v_shared.reshape(d, n_kv * hd)
            bk, bv = b_k_shared.reshape(1, -1), b_v_shared.reshape(1, -1)
            wkv = jnp.concatenate([wk, wv_s, swap_pairs(wk), swap_pairs(wv_s)], axis=1).astype(BF16)
            bkv = jnp.concatenate([bk, bv, swap_pairs(bk), swap_pairs(bv)], axis=1).astype(F32)
            (q_p, kv_p_new, kvb_p), (q_s, kv_s_new, _) = [
                _swa_in(h, ga, vec(kv_norm_shared), wq, vec(swa_bq[b]), wkv, bkv, scale=swa_scale) for h in streams]
            if b == 0:
                kv_p, kv_pb, kv_s = kv_p_new, kvb_p, kv_s_new
                state = jnp.transpose(state_kv_win, (0, 2, 3, 4, 1)).reshape(dec_batch, 2 * n_kv * hd, n_win)
            sinks = swa_sinks[b].astype(F32)

            bias_p = _band_bias(rel_bias).at[:, :, 0].set(sinks[:, None])
            pairs = group // 2
            bias_p = bias_p.reshape(n_kv, pairs, 2, WINDOW, 2 * WINDOW)
            bias_p = bias_p.transpose(0, 1, 3, 2, 4).reshape(n_kv, pairs * WINDOW, 4 * WINDOW)
            o_p = _swa_prompt_attn(q_p, kv_pb, bias_p, batch=batch, seq=seq, n_kv=n_kv, group=group, hd=hd)

            n_keys = 2 * n_win
            t_idx = jnp.arange(dec_seq, dtype=jnp.int32)[:, None]
            j_idx = jnp.arange(n_keys, dtype=jnp.int32)[None, :]
            c_idx = jnp.where(j_idx < n_win, j_idx, j_idx - (n_win - dec_seq))
            real = (j_idx < n_win) | (j_idx >= n_keys - dec_seq)
            dist_s = jnp.where(real, n_win + t_idx - c_idx, -1)
            bias_s = _window_bias(rel_bias, dist_s)
            bias_s = bias_s.reshape(n_kv, group * dec_seq, n_keys)
            sink_s = jnp.repeat(sinks, dec_seq).reshape(n_kv, group * dec_seq, 1)
            q4 = q_s.reshape(dec_batch, dec_seq, n_kv, group, hd).transpose(0, 2, 3, 1, 4).reshape(dec_batch, n_kv, group * dec_seq, hd)
            new_pad = jnp.pad(kv_s.reshape(dec_batch, dec_seq, -1), ((0, 0), (8 - dec_seq, 0), (0, 0)))
            o4, win = _swa_sample_attn(q4, state, new_pad, bias_s, sink_s, dec_seq=dec_seq)
            if b == 0:
                kv_win_sample = win
            o_s = o4.reshape(dec_batch, n_kv, group, dec_seq, hd).transpose(0, 3, 1, 2, 4).reshape(dec_batch * dec_seq, heads_b * hd)
            wo = swa_wo[b].reshape(heads_b * hd, d).astype(BF16)
            streams = [_proj_residual(o, wo, vec(swa_bo[b]), h) for o, h in zip((o_p, o_s), streams)]

        last = layer == depth - 1
        streams = [_ffn(h, vec(ffn_norm[layer]), ffn_wg, ffn_wu, ffn_wd, vec(final_norm), layer, final_norm=last)
                   for h in streams]

    y_prompt = streams[0].reshape(batch, seq, d)
    y_sample = streams[1].reshape(dec_batch, dec_seq, d)
    n_tail = min(WINDOW, seq)
    kv_win_prompt = kv_p.reshape(batch, seq, -1)[:, seq - n_tail:].reshape(batch, n_tail, 2, n_kv, hd)
    kv_win_sample = jnp.transpose(kv_win_sample.reshape(dec_batch, 2, n_kv, hd, n_win), (0, 4, 1, 2, 3))
    return (y_prompt, y_sample, jnp.stack(rows_p_all, axis=0), jnp.stack(rows_s_all, axis=0),
            kv_win_prompt, kv_win_sample)
```

```python
import functools
import math

import jax
import jax.numpy as jnp
from jax import lax
from jax.experimental import pallas as pl
from jax.experimental.pallas import tpu as pltpu

F32 = jnp.float32
BF16 = jnp.bfloat16

EPS = 1e-6
NEG_INF = -1e30
ROPE_THETA = 10000.0
PAGE_SIZE = 128
WINDOW = 128
N_BUCKETS = 32
MAX_DISTANCE = 128

VMEM_LIMIT_BYTES = 56 * 1024 * 1024
ROW_TILE = 512
FFN_TILE = 512
FFN_CHUNKS = 2
MLA_Q_TILE = 256
MLA_K_TILE = 512
MLA_HEADS_PER_PASS = 2
PAGES_PER_STEP = 64
SAMPLE_CHAINS = 4
NEW_ROWS_PAD = 16
SWA_SEQS_PER_STEP = 8


def _params(*semantics):
    return pltpu.CompilerParams(dimension_semantics=semantics, vmem_limit_bytes=VMEM_LIMIT_BYTES)


def _rms(x, g):
    return x * lax.rsqrt(jnp.mean(x * x, axis=-1, keepdims=True) + EPS) * g


def _dot(a, b):
    return jnp.dot(a, b, preferred_element_type=F32)


def _dot_nt(a, b):
    return lax.dot_general(a, b, (((1,), (1,)), ((), ())), preferred_element_type=F32)


def _resident(shape):
    return pl.BlockSpec(shape, lambda *_: (0,) * len(shape))


def _table_index(table_rows, tm):
    assert table_rows % tm == 0
    blocks = table_rows // tm
    return lambda i: (i % blocks, 0)


def _mla_in_kernel(h_ref, g_ref, w_ref, qn_ref, kvn_ref, cos_ref, sin_ref,
                   cq_ref, rows_ref, rowsb_ref, *, q_lora, kv_lora, rope):
    xn = _rms(h_ref[...], g_ref[...]).astype(BF16)
    y = _dot(xn, w_ref[...])
    cq_ref[...] = _rms(y[:, :q_lora], qn_ref[...]).astype(BF16)
    c0 = q_lora
    ckv = _rms(y[:, c0:c0 + kv_lora], kvn_ref[...])
    p0 = c0 + kv_lora
    kpe = y[:, p0:p0 + rope] * cos_ref[...] + y[:, p0 + rope:p0 + 2 * rope] * sin_ref[...]
    rows_ref[:, :kv_lora] = ckv
    rows_ref[:, kv_lora:] = kpe
    rowsb_ref[:, :kv_lora] = ckv.astype(BF16)
    rowsb_ref[:, kv_lora:] = kpe.astype(BF16)


def _mla_in(h, g, w, qn, kvn, cos2, sin2, *, q_lora, kv_lora, rope):
    m, d = h.shape
    tm = min(ROW_TILE, m)
    lat = kv_lora + rope
    row = lambda i: (i, 0)
    tab = _table_index(cos2.shape[0], tm)
    return pl.pallas_call(
        functools.partial(_mla_in_kernel, q_lora=q_lora, kv_lora=kv_lora, rope=rope),
        grid=(m // tm,),
        in_specs=[pl.BlockSpec((tm, d), row), _resident((1, d)), _resident(w.shape),
                  _resident((1, q_lora)), _resident((1, kv_lora)),
                  pl.BlockSpec((tm, rope), tab), pl.BlockSpec((tm, rope), tab)],
        out_specs=[pl.BlockSpec((tm, q_lora), row), pl.BlockSpec((tm, lat), row), pl.BlockSpec((tm, lat), row)],
        out_shape=[jax.ShapeDtypeStruct((m, q_lora), BF16), jax.ShapeDtypeStruct((m, lat), F32),
                   jax.ShapeDtypeStruct((m, lat), BF16)],
        compiler_params=_params("arbitrary"),
        name="mla_in",
    )(h, g, w, qn, kvn, cos2, sin2)


def _mla_q_kernel(cq_ref, wqb_ref, wkt_ref, cos_ref, sin_ref, q_ref, *, heads, nope, rope, kv_lora, scale):
    y = _dot(cq_ref[...], wqb_ref[...])
    cos = cos_ref[...]
    sin = sin_ref[...]
    pe0 = heads * nope
    rot0 = pe0 + heads * rope
    for h in range(heads):
        qn = y[:, h * nope:(h + 1) * nope].astype(BF16)
        q_ref[h, :, :kv_lora] = (_dot(qn, wkt_ref[h]) * scale).astype(BF16)
        pe = y[:, pe0 + h * rope:pe0 + (h + 1) * rope] * cos + y[:, rot0 + h * rope:rot0 + (h + 1) * rope] * sin
        q_ref[h, :, kv_lora:] = (pe * scale).astype(BF16)


def _mla_q(cq, wqb, wkt, cos2, sin2, *, scale):
    m, q_lora = cq.shape
    heads, nope, kv_lora = wkt.shape
    rope = cos2.shape[1]
    tm = min(256, m)
    row = lambda i: (i, 0)
    tab = _table_index(cos2.shape[0], tm)
    return pl.pallas_call(
        functools.partial(_mla_q_kernel, heads=heads, nope=nope, rope=rope, kv_lora=kv_lora, scale=scale),
        grid=(m // tm,),
        in_specs=[pl.BlockSpec((tm, q_lora), row), _resident(wqb.shape), _resident(wkt.shape),
                  pl.BlockSpec((tm, rope), tab), pl.BlockSpec((tm, rope), tab)],
        out_specs=pl.BlockSpec((heads, tm, kv_lora + rope), lambda i: (0, i, 0)),
        out_shape=jax.ShapeDtypeStruct((heads, m, kv_lora + rope), BF16),
        compiler_params=_params("arbitrary"),
        name="mla_q",
    )(cq, wqb, wkt, cos2, sin2)


def _softmax_step(sc, pv, m_sc, l_sc, acc_sc, r=slice(None)):
    m_prev = m_sc[r]
    m_new = jnp.maximum(m_prev, jnp.max(sc, axis=1, keepdims=True))
    alpha = jnp.exp(m_prev - m_new)
    p = jnp.exp(sc - m_new)
    l_sc[r] = alpha * l_sc[r] + jnp.sum(p, axis=1, keepdims=True)
    acc_sc[r] = alpha * acc_sc[r] + pv(p.astype(BF16))
    m_sc[r] = m_new


def _mla_prompt_kernel(qi_ref, kj_ref, q_ref, k_ref, o_ref, m_sc, l_sc, acc_sc, *, heads, tq, tk, kv_lora):
    s = pl.program_id(1)
    qi = qi_ref[s]
    kj = kj_ref[s]
    diag = (qi * tq + tq - 1) // tk
    hp = MLA_HEADS_PER_PASS
    rows = hp * tq

    @pl.when(kj == 0)
    def _():
        m_sc[...] = jnp.full_like(m_sc, NEG_INF)
        l_sc[...] = jnp.zeros_like(l_sc)
        acc_sc[...] = jnp.zeros_like(acc_sc)

    def step(mask):
        k = k_ref[...]
        v = k[:, :kv_lora]

        def scores(g):
            sc = _dot_nt(q_ref[g * hp:(g + 1) * hp].reshape(rows, q_ref.shape[-1]), k)
            return sc if mask is None else jnp.where(mask, sc, NEG_INF)

        n_groups = heads // hp
        sc = scores(0)
        for g in range(n_groups):
            nxt = scores(g + 1) if g + 1 < n_groups else None
            _softmax_step(sc, lambda p: _dot(p, v), m_sc, l_sc, acc_sc, slice(g * rows, (g + 1) * rows))
            sc = nxt

    @pl.when(kj != diag)
    def _():
        step(None)

    @pl.when(kj == diag)
    def _():
        q_pos = qi * tq + (lax.broadcasted_iota(jnp.int32, (rows, tk), 0) & (tq - 1))
        k_pos = kj * tk + lax.broadcasted_iota(jnp.int32, (rows, tk), 1)
        step(k_pos <= q_pos)
        o_ref[...] = (acc_sc[...] / l_sc[...]).astype(BF16).reshape(o_ref.shape)


def _mla_prompt_attn(q_full, rows_b, *, batch, seq, kv_lora):
    heads, _, lat = q_full.shape
    tq, tk = MLA_Q_TILE, MLA_K_TILE
    assert tq & (tq - 1) == 0 and seq % tq == 0 and seq % tk == 0 and tk % tq == 0
    nq, nk = seq // tq, seq // tk
    qi_list, kj_list = [], []
    for qi in range(nq):
        for kj in range((qi * tq + tq - 1) // tk + 1):
            qi_list.append(qi)
            kj_list.append(kj)
    qi_tab = jnp.asarray(qi_list, jnp.int32)
    kj_tab = jnp.asarray(kj_list, jnp.int32)
    rows = heads * tq
    grid_spec = pltpu.PrefetchScalarGridSpec(
        num_scalar_prefetch=2,
        grid=(batch, len(qi_list)),
        in_specs=[pl.BlockSpec((heads, tq, lat), lambda b, s, qi, kj: (0, b * nq + qi[s], 0)),
                  pl.BlockSpec((tk, lat), lambda b, s, qi, kj: (b * nk + kj[s], 0))],
        out_specs=pl.BlockSpec((heads, tq, kv_lora), lambda b, s, qi, kj: (0, b * nq + qi[s], 0)),
        scratch_shapes=[pltpu.VMEM((rows, 1), F32), pltpu.VMEM((rows, 1), F32), pltpu.VMEM((rows, kv_lora), F32)],
    )
    return pl.pallas_call(
        functools.partial(_mla_prompt_kernel, heads=heads, tq=tq, tk=tk, kv_lora=kv_lora),
        grid_spec=grid_spec,
        out_shape=jax.ShapeDtypeStruct((heads, batch * seq, kv_lora), BF16),
        compiler_params=_params("arbitrary", "arbitrary"),
        name="mla_prompt_attn",
    )(qi_tab, kj_tab, q_full, rows_b)


def _mla_sample_kernel(pt_ref, q_ref, new_ref, *refs, n_pages, dec_seq, kv_lora):
    page_refs = refs[:n_pages]
    o_ref, m_sc, l_sc, acc_sc = refs[n_pages:]
    g = pl.program_id(1)

    @pl.when(g == 0)
    def _():
        m_sc[...] = jnp.full_like(m_sc, NEG_INF)
        l_sc[...] = jnp.zeros_like(l_sc)
        acc_sc[...] = jnp.zeros_like(acc_sc)

    q = q_ref[...]
    chains = m_sc.shape[0]
    per = n_pages // chains
    kts, scs = [], []
    for c in range(chains):
        kts.append(jnp.concatenate([r[...].astype(BF16) for r in page_refs[c * per:(c + 1) * per]], axis=1))
        scs.append(_dot(q, kts[c]))
    for c in range(chains):
        vt = kts[c][:kv_lora]
        _softmax_step(scs[c], lambda p, vt=vt: _dot_nt(p, vt), m_sc, l_sc, acc_sc, c)

    @pl.when(g == pl.num_programs(1) - 1)
    def _():
        kn = new_ref[...]
        sc = _dot_nt(q, kn)
        t = lax.broadcasted_iota(jnp.int32, sc.shape, 0) % dec_seq
        c = lax.broadcasted_iota(jnp.int32, sc.shape, 1)
        _softmax_step(jnp.where(c <= t, sc, NEG_INF), lambda p: _dot(p, kn[:, :kv_lora]), m_sc, l_sc, acc_sc, 0)
        m = functools.reduce(jnp.maximum, [m_sc[c] for c in range(chains)])
        weights = [jnp.exp(m_sc[c] - m) for c in range(chains)]
        l = sum(l_sc[c] * weights[c] for c in range(chains))
        acc = sum(acc_sc[c] * weights[c] for c in range(chains))
        o_ref[...] = (acc / l).astype(BF16)


def _mla_sample_attn(q_s, new_pad, cache_t, layer, page_table, *, dec_seq, kv_lora):
    n_seq, rows, lat = q_s.shape
    n_pages_total = page_table.shape[1]
    p = PAGES_PER_STEP
    assert n_pages_total % p == 0 and p % SAMPLE_CHAINS == 0 and cache_t.shape[2:] == (lat, PAGE_SIZE)
    page_specs = [
        pl.BlockSpec((None, None, lat, PAGE_SIZE), lambda s, g, pt, j=j: (layer, pt[s, g * p + j], 0, 0))
        for j in range(p)
    ]
    grid_spec = pltpu.PrefetchScalarGridSpec(
        num_scalar_prefetch=1,
        grid=(n_seq, n_pages_total // p),
        in_specs=[pl.BlockSpec((None, rows, lat), lambda s, g, pt: (s, 0, 0)),
                  pl.BlockSpec((None, NEW_ROWS_PAD, lat), lambda s, g, pt: (s, 0, 0))] + page_specs,
        out_specs=pl.BlockSpec((None, rows, kv_lora), lambda s, g, pt: (s, 0, 0)),
        scratch_shapes=[pltpu.VMEM((SAMPLE_CHAINS, rows, 1), F32), pltpu.VMEM((SAMPLE_CHAINS, rows, 1), F32),
                        pltpu.VMEM((SAMPLE_CHAINS, rows, kv_lora), F32)],
    )
    return pl.pallas_call(
        functools.partial(_mla_sample_kernel, n_pages=p, dec_seq=dec_seq, kv_lora=kv_lora),
        grid_spec=grid_spec,
        out_shape=jax.ShapeDtypeStruct((n_seq, rows, kv_lora), BF16),
        compiler_params=_params("arbitrary", "arbitrary"),
        name="mla_sample_attn",
    )(page_table, q_s, new_pad, *([cache_t] * p))


def _mla_out_kernel(o_ref, wv_ref, wo_ref, h_ref, out_ref, *, heads):
    v = jnp.concatenate([_dot(o_ref[h], wv_ref[h]).astype(BF16) for h in range(heads)], axis=1)
    out_ref[...] = h_ref[...] + _dot(v, wo_ref[...])


def _mla_out(o_lat, wv, wo, h):
    heads, m, kv_lora = o_lat.shape
    d = h.shape[1]
    tm = min(256, m)
    row = lambda i: (i, 0)
    return pl.pallas_call(
        functools.partial(_mla_out_kernel, heads=heads),
        grid=(m // tm,),
        in_specs=[pl.BlockSpec((heads, tm, kv_lora), lambda i: (0, i, 0)), _resident(wv.shape), _resident(wo.shape),
                  pl.BlockSpec((tm, d), row)],
        out_specs=pl.BlockSpec((tm, d), row),
        out_shape=jax.ShapeDtypeStruct((m, d), F32),
        compiler_params=_params("arbitrary"),
        name="mla_out",
    )(o_lat, wv, wo, h)


def _ffn_kernel(h_ref, g_ref, wg_ref, wu_ref, wd_ref, fg_ref, out_ref, xn_sc, acc_sc, *, final_norm):
    j = pl.program_id(1)

    @pl.when(j == 0)
    def _():
        xn_sc[...] = _rms(h_ref[...], g_ref[...]).astype(BF16)
        acc_sc[...] = jnp.zeros_like(acc_sc)

    xn = xn_sc[...]
    chunk = wg_ref.shape[1] // FFN_CHUNKS

    def gate_up(c):
        cols = slice(c * chunk, (c + 1) * chunk)
        return _dot(xn, wg_ref[:, cols]), _dot(xn, wu_ref[:, cols])

    cur = gate_up(0)
    total = None
    for c in range(FFN_CHUNKS):
        nxt = gate_up(c + 1) if c + 1 < FFN_CHUNKS else None
        a, u = cur
        part = _dot((a * jax.nn.sigmoid(a) * u).astype(BF16), wd_ref[c * chunk:(c + 1) * chunk, :])
        total = part if total is None else total + part
        cur = nxt
    acc_sc[...] += total

    @pl.when(j == pl.num_programs(1) - 1)
    def _():
        res = h_ref[...] + acc_sc[...]
        out_ref[...] = _rms(res, fg_ref[...]) if final_norm else res


def _ffn(h, g, wg, wu, wd, fg, layer, *, final_norm):
    m, d = h.shape
    dff = wg.shape[2]
    tm = min(ROW_TILE, m)
    tf = FFN_TILE
    assert dff % tf == 0
    row = lambda i, j: (i, 0)
    return pl.pallas_call(
        functools.partial(_ffn_kernel, final_norm=final_norm),
        grid=(m // tm, dff // tf),
        in_specs=[pl.BlockSpec((tm, d), row), _resident((1, d)),
                  pl.BlockSpec((None, d, tf), lambda i, j: (layer, 0, j)),
                  pl.BlockSpec((None, d, tf), lambda i, j: (layer, 0, j)),
                  pl.BlockSpec((None, tf, d), lambda i, j: (layer, j, 0)), _resident((1, d))],
        out_specs=pl.BlockSpec((tm, d), row),
        out_shape=jax.ShapeDtypeStruct((m, d), F32),
        scratch_shapes=[pltpu.VMEM((tm, d), BF16), pltpu.VMEM((tm, d), F32)],
        compiler_params=_params("arbitrary", "arbitrary"),
        name="ffn",
    )(h, g, wg, wu, wd, fg)


def _swa_in_kernel(h_ref, ga_ref, gk_ref, wq_ref, bq_ref, wkv_ref, bkv_ref, q_ref, kv_ref, kvb_ref, *, scale):
    x = h_ref[...]
    y = x * lax.rsqrt(jnp.mean(x * x, axis=-1, keepdims=True) + EPS)
    xn = (y * ga_ref[...]).astype(BF16)
    hn = (y * gk_ref[...]).astype(BF16)
    q_ref[...] = ((_dot(xn, wq_ref[...]) + bq_ref[...]) * scale).astype(BF16)
    kv = _dot(hn, wkv_ref[...]) + bkv_ref[...]
    kv_ref[...] = kv[:, :kv_ref.shape[1]]
    kvb_ref[...] = kv.astype(BF16)


def _swa_in(h, ga, gk, wq, bq, wkv, bkv, *, scale):
    m, d = h.shape
    nq, nkv = wq.shape[1], wkv.shape[1]
    tm = min(256, m)
    row = lambda i: (i, 0)
    return pl.pallas_call(
        functools.partial(_swa_in_kernel, scale=scale),
        grid=(m // tm,),
        in_specs=[pl.BlockSpec((tm, d), row), _resident((1, d)), _resident((1, d)), _resident(wq.shape),
                  _resident((1, nq)), _resident(wkv.shape), _resident((1, nkv))],
        out_specs=[pl.BlockSpec((tm, nq), row), pl.BlockSpec((tm, nkv // 2), row), pl.BlockSpec((tm, nkv), row)],
        out_shape=[jax.ShapeDtypeStruct((m, nq), BF16), jax.ShapeDtypeStruct((m, nkv // 2), F32),
                   jax.ShapeDtypeStruct((m, nkv), BF16)],
        compiler_params=_params("arbitrary"),
        name="swa_in",
    )(h, ga, gk, wq, bq, wkv, bkv)


def _swa_prompt_kernel(q_ref, prev_ref, cur_ref, bias_ref, o_ref, *, n_kv, group, hd):
    i = pl.program_id(1)
    w = q_ref.shape[0]
    lanes = 2 * hd
    pairs = group // 2
    band = 2 * w
    q = q_ref[...]
    kv = jnp.concatenate([prev_ref[...], cur_ref[...]], axis=0)
    col = lax.broadcasted_iota(jnp.int32, (1, 2 * band), 1) & (band - 1)
    live = (col >= w) | (i > 0) | (col == 0)
    low = lax.broadcasted_iota(jnp.int32, (band, lanes), 1) < hd
    real = lax.broadcasted_iota(jnp.int32, (band, lanes), 0) > 0
    keep_first, keep_second = low & real, (~low) & real
    ones = jnp.concatenate([low, ~low], axis=0).astype(BF16)
    zero = jnp.zeros((), BF16)
    swapped = n_kv * lanes

    def block_diag(slab0, n):
        nat = kv[:, slab0 + (n // 2) * lanes:slab0 + (n // 2 + 1) * lanes]
        swp = kv[:, swapped + slab0 + (n // 2) * lanes:swapped + slab0 + (n // 2 + 1) * lanes]
        first, second = (nat, swp) if n % 2 == 0 else (swp, nat)
        return jnp.concatenate([jnp.where(keep_first, first, zero), jnp.where(keep_second, second, zero)], axis=0)

    for n in range(n_kv):
        k2 = block_diag(0, n)
        v2 = jnp.concatenate([block_diag(n_kv * hd, n), ones], axis=1)
        qn = jnp.concatenate([q[:, (n * pairs + p) * lanes:(n * pairs + p + 1) * lanes] for p in range(pairs)], axis=0)
        s = jnp.where(live, _dot_nt(qn, k2) + bias_ref[n], NEG_INF)
        probs = []
        for half in range(2):
            sh = s[:, half * band:(half + 1) * band]
            probs.append(jnp.exp(sh - jnp.max(sh, axis=1, keepdims=True)).astype(BF16))
        pv = _dot(jnp.concatenate(probs, axis=1), v2)
        o = pv[:, :lanes] / pv[:, lanes:]
        for p in range(pairs):
            o_ref[:, (n * pairs + p) * lanes:(n * pairs + p + 1) * lanes] = o[p * w:(p + 1) * w].astype(BF16)


def _swa_prompt_attn(q, kvb, bias, *, batch, seq, n_kv, group, hd):
    w = WINDOW
    nb = seq // w
    assert group % 2 == 0 and kvb.shape[1] == 4 * n_kv * hd
    return pl.pallas_call(
        functools.partial(_swa_prompt_kernel, n_kv=n_kv, group=group, hd=hd),
        grid=(batch, nb),
        in_specs=[pl.BlockSpec((w, q.shape[1]), lambda b, i: (b * nb + i, 0)),
                  pl.BlockSpec((w, kvb.shape[1]), lambda b, i: (b * nb + jnp.maximum(i - 1, 0), 0)),
                  pl.BlockSpec((w, kvb.shape[1]), lambda b, i: (b * nb + i, 0)),
                  _resident(bias.shape)],
        out_specs=pl.BlockSpec((w, q.shape[1]), lambda b, i: (b * nb + i, 0)),
        out_shape=jax.ShapeDtypeStruct(q.shape, BF16),
        compiler_params=_params("arbitrary", "arbitrary"),
        name="swa_prompt_attn",
    )(q, kvb, kvb, bias)


def _swa_sample_kernel(q_ref, state_ref, new_ref, bias_ref, sink_ref, o_ref, win_ref, *, n_kv, hd, dec_seq):
    n_win = state_ref.shape[2]
    lane = lax.broadcasted_iota(jnp.int32, (1, n_win), 1)
    for sb in range(q_ref.shape[0]):
        state_t = state_ref[sb]
        new = new_ref[sb]
        lead = jnp.zeros((n_win - new.shape[0], new.shape[1]), F32)
        new_t = jnp.concatenate([lead, new], axis=0).T
        win_ref[sb] = jnp.where(lane < n_win - dec_seq, pltpu.roll(state_t, n_win - dec_seq, axis=1), new_t)
        kv_t = jnp.concatenate([state_t, new_t], axis=1).astype(BF16)
        for n in range(n_kv):
            kt = kv_t[n * hd:(n + 1) * hd]
            vt = kv_t[(n_kv + n) * hd:(n_kv + n + 1) * hd]
            s = _dot(q_ref[sb, n], kt) + bias_ref[n]
            sink = sink_ref[n]
            m = jnp.maximum(jnp.max(s, axis=1, keepdims=True), sink)
            p = jnp.exp(s - m)
            denom = jnp.sum(p, axis=1, keepdims=True) + jnp.exp(sink - m)
            o_ref[sb, n] = (_dot_nt(p.astype(BF16), vt) / denom).astype(BF16)


def _swa_sample_attn(q4, state_t, new_pad, bias, sink_rows, *, dec_seq):
    n_seq, n_kv, rows, hd = q4.shape
    width, n_win = state_t.shape[1], state_t.shape[2]
    sb = SWA_SEQS_PER_STEP
    assert n_seq % sb == 0 and bias.shape[2] == 2 * n_win
    blk = lambda i: (i, 0, 0)
    return pl.pallas_call(
        functools.partial(_swa_sample_kernel, n_kv=n_kv, hd=hd, dec_seq=dec_seq),
        grid=(n_seq // sb,),
        in_specs=[pl.BlockSpec((sb, n_kv, rows, hd), lambda i: (i, 0, 0, 0)),
                  pl.BlockSpec((sb, width, n_win), blk), pl.BlockSpec((sb,) + new_pad.shape[1:], blk),
                  _resident(bias.shape), _resident(sink_rows.shape)],
        out_specs=[pl.BlockSpec((sb, n_kv, rows, hd), lambda i: (i, 0, 0, 0)), pl.BlockSpec((sb, width, n_win), blk)],
        out_shape=[jax.ShapeDtypeStruct(q4.shape, BF16), jax.ShapeDtypeStruct(state_t.shape, F32)],
        compiler_params=_params("arbitrary"),
        name="swa_sample_attn",
    )(q4, state_t, new_pad, bias, sink_rows)


def _proj_residual_kernel(o_ref, w_ref, b_ref, h_ref, out_ref):
    out_ref[...] = h_ref[...] + (_dot(o_ref[...], w_ref[...]) + b_ref[...])


def _proj_residual(o, w, b, h):
    m, d = h.shape
    tm = min(ROW_TILE, m)
    row = lambda i: (i, 0)
    return pl.pallas_call(
        _proj_residual_kernel,
        grid=(m // tm,),
        in_specs=[pl.BlockSpec((tm, o.shape[1]), row), _resident(w.shape), _resident((1, d)), pl.BlockSpec((tm, d), row)],
        out_specs=pl.BlockSpec((tm, d), row),
        out_shape=jax.ShapeDtypeStruct((m, d), F32),
        compiler_params=_params("arbitrary"),
        name="swa_out",
    )(o, w, b, h)


def _rope_tables(pos, rope):
    inv_freq = ROPE_THETA ** (-jnp.arange(0, rope, 2, dtype=F32) / rope)
    ang = pos.astype(F32)[:, None] * inv_freq[None, :]
    cos, sin = jnp.cos(ang), jnp.sin(ang)
    return jnp.concatenate([cos, cos], axis=1), jnp.concatenate([sin, sin], axis=1)


def _rot_half_cols(w):
    half = w.shape[-1] // 2
    return jnp.concatenate([-w[..., half:], w[..., :half]], axis=-1)


def _bucket(dist):
    n = jnp.maximum(dist, 0)
    max_exact = N_BUCKETS // 2
    nf = jnp.maximum(n, 1).astype(F32)
    large = max_exact + (jnp.log(nf / max_exact) / math.log(MAX_DISTANCE / max_exact)
                         * (N_BUCKETS - max_exact)).astype(jnp.int32)
    return jnp.where(n < max_exact, n, jnp.minimum(large, N_BUCKETS - 1))


def _window_bias(rel_bias, dist):
    valid = (dist >= 0) & (dist < WINDOW)
    bias = jnp.moveaxis(jnp.take(rel_bias.astype(F32), _bucket(dist), axis=0), -1, 0)
    return jnp.where(valid[None], bias, NEG_INF)


def _band_bias(rel_bias):
    w = WINDOW
    per_dist = _window_bias(rel_bias, jnp.arange(w, dtype=jnp.int32)[None, :])[:, 0, :]
    heads = per_dist.shape[0]
    neg = lambda n: jnp.full((heads, n), NEG_INF, F32)
    line = jnp.concatenate([neg(w), per_dist[:, ::-1], neg(w)], axis=1)
    skew = jnp.tile(line, (1, w))[:, :w * (3 * w - 1)].reshape(heads, w, 3 * w - 1)
    return skew[:, :, w - 1:3 * w - 1]


def kernel(x_prompt, x_sample, cache_mla, state_kv_win, page_table, attn_norm, ffn_norm, final_norm, mla_wq_a, mla_q_norm, mla_wq_b, mla_wkv_a, mla_kv_norm, mla_wkv_b, mla_wo, kv_norm_shared, w_k_shared, b_k_shared, w_v_shared, b_v_shared, swa_wq, swa_bq, swa_sinks, swa_wo, swa_bo, rel_bias, ffn_w_gate, ffn_w_up, ffn_w_down):
    batch, seq, d = x_prompt.shape
    dec_batch, dec_seq, _ = x_sample.shape
    depth = attn_norm.shape[0]
    n_a = mla_wq_a.shape[0]
    past_len = page_table.shape[1] * PAGE_SIZE
    n_win = state_kv_win.shape[1]
    q_lora = mla_wq_a.shape[2]
    kv_lora = mla_kv_norm.shape[1]
    lat = mla_wkv_a.shape[2]
    rope = lat - kv_lora
    heads_a = mla_wq_b.shape[2]
    nope = mla_wq_b.shape[3] - rope
    heads_b, hd = swa_wq.shape[2], swa_wq.shape[3]
    n_kv = w_k_shared.shape[1]
    group = heads_b // n_kv
    assert seq % WINDOW == 0 and n_win == WINDOW and dec_seq <= 8
    mla_scale = 1.0 / math.sqrt(nope + rope)
    swa_scale = 1.0 / math.sqrt(hd)

    vec = lambda a: a.reshape(1, -1).astype(F32)
    streams = [x_prompt.reshape(batch * seq, d), x_sample.reshape(dec_batch * dec_seq, d)]
    rope_tabs = [_rope_tables(jnp.arange(seq, dtype=jnp.int32), rope),
                 _rope_tables(jnp.tile(past_len + jnp.arange(dec_seq, dtype=jnp.int32), dec_batch), rope)]

    cache_t = jnp.swapaxes(cache_mla, 2, 3)
    ffn_wg, ffn_wu, ffn_wd = ffn_w_gate.astype(BF16), ffn_w_up.astype(BF16), ffn_w_down.astype(BF16)

    rows_p_all, rows_s_all = [], []
    kv_p = kv_win_sample = None
    for layer in range(depth):
        ga = vec(attn_norm[layer])
        if layer < n_a:
            a = layer
            wkv_a = mla_wkv_a[a]
            w_in = jnp.concatenate([mla_wq_a[a], wkv_a, _rot_half_cols(wkv_a[:, kv_lora:])], axis=1).astype(BF16)
            wq_pe = mla_wq_b[a][:, :, nope:]
            wqb = jnp.concatenate([mla_wq_b[a][:, :, :nope].reshape(q_lora, heads_a * nope),
                                   wq_pe.reshape(q_lora, heads_a * rope),
                                   _rot_half_cols(wq_pe).reshape(q_lora, heads_a * rope)], axis=1).astype(BF16)
            wkt = jnp.transpose(mla_wkv_b[a][:, :, :nope], (1, 2, 0)).astype(BF16)
            wv = jnp.transpose(mla_wkv_b[a][:, :, nope:], (1, 0, 2)).astype(BF16)
            wo = mla_wo[a].reshape(-1, d).astype(BF16)

            stage = []
            for h, (cos2, sin2) in zip(streams, rope_tabs):
                cq, rows, rows_b = _mla_in(h, ga, w_in, vec(mla_q_norm[a]), vec(mla_kv_norm[a]), cos2, sin2,
                                           q_lora=q_lora, kv_lora=kv_lora, rope=rope)
                stage.append((_mla_q(cq, wqb, wkt, cos2, sin2, scale=mla_scale), rows, rows_b))
            (q_p, rows_p, rows_pb), (q_s, rows_s, rows_sb) = stage
            rows_p_all.append(rows_p.reshape(batch, seq, lat))
            rows_s_all.append(rows_s.reshape(dec_batch, dec_seq, lat))

            o_p = _mla_prompt_attn(q_p, rows_pb, batch=batch, seq=seq, kv_lora=kv_lora)
            q_s = q_s.reshape(heads_a, dec_batch, dec_seq, lat).transpose(1, 0, 2, 3).reshape(dec_batch, heads_a * dec_seq, lat)
            new_pad = jnp.pad(rows_sb.reshape(dec_batch, dec_seq, lat), ((0, 0), (0, NEW_ROWS_PAD - dec_seq), (0, 0)))
            o_s = _mla_sample_attn(q_s, new_pad, cache_t, a, page_table, dec_seq=dec_seq, kv_lora=kv_lora)
            o_s = o_s.reshape(dec_batch, heads_a, dec_seq, kv_lora).transpose(1, 0, 2, 3).reshape(heads_a, dec_batch * dec_seq, kv_lora)
            streams = [_mla_out(o, wv, wo, h) for o, h in zip((o_p, o_s), streams)]
        else:
            b = layer - n_a
            wq = swa_wq[b].reshape(d, heads_b * hd).astype(BF16)
            swap_pairs = lambda t: t.reshape(t.shape[0], n_kv // 2, 2, hd)[:, :, ::-1].reshape(t.shape[0], n_kv * hd)
            wk, wv_s = w_k_shared.reshape(d, n_kv * hd), w_v_shared.reshape(d, n_kv * hd)
            bk, bv = b_k_shared.reshape(1, -1), b_v_shared.reshape(1, -1)
            wkv = jnp.concatenate([wk, wv_s, swap_pairs(wk), swap_pairs(wv_s)], axis=1).astype(BF16)
            bkv = jnp.concatenate([bk, bv, swap_pairs(bk), swap_pairs(bv)], axis=1).astype(F32)
            (q_p, kv_p_new, kvb_p), (q_s, kv_s_new, _) = [
                _swa_in(h, ga, vec(kv_norm_shared), wq, vec(swa_bq[b]), wkv, bkv, scale=swa_scale) for h in streams]
            if b == 0:
                kv_p, kv_pb, kv_s = kv_p_new, kvb_p, kv_s_new
                state = jnp.transpose(state_kv_win, (0, 2, 3, 4, 1)).reshape(dec_batch, 2 * n_kv * hd, n_win)
            sinks = swa_sinks[b].astype(F32)

            bias_p = _band_bias(rel_bias).at[:, :, 0].set(sinks[:, None])
            pairs = group // 2
            bias_p = bias_p.reshape(n_kv, pairs, 2, WINDOW, 2 * WINDOW)
            bias_p = bias_p.transpose(0, 1, 3, 2, 4).reshape(n_kv, pairs * WINDOW, 4 * WINDOW)
            o_p = _swa_prompt_attn(q_p, kv_pb, bias_p, batch=batch, seq=seq, n_kv=n_kv, group=group, hd=hd)

            n_keys = 2 * n_win
            t_idx = jnp.arange(dec_seq, dtype=jnp.int32)[:, None]
            j_idx = jnp.arange(n_keys, dtype=jnp.int32)[None, :]
            c_idx = jnp.where(j_idx < n_win, j_idx, j_idx - (n_win - dec_seq))
            real = (j_idx < n_win) | (j_idx >= n_keys - dec_seq)
            dist_s = jnp.where(real, n_win + t_idx - c_idx, -1)
            bias_s = _window_bias(rel_bias, dist_s)
            bias_s = bias_s.reshape(n_kv, group * dec_seq, n_keys)
            sink_s = jnp.repeat(sinks, dec_seq).reshape(n_kv, group * dec_seq, 1)
            q4 = q_s.reshape(dec_batch, dec_seq, n_kv, group, hd).transpose(0, 2, 3, 1, 4).reshape(dec_batch, n_kv, group * dec_seq, hd)
            new_pad = jnp.pad(kv_s.reshape(dec_batch, dec_seq, -1), ((0, 0), (8 - dec_seq, 0), (0, 0)))
            o4, win = _swa_sample_attn(q4, state, new_pad, bias_s, sink_s, dec_seq=dec_seq)
            if b == 0:
                kv_win_sample = win
            o_s = o4.reshape(dec_batch, n_kv, group, dec_seq, hd).transpose(0, 3, 1, 2, 4).reshape(dec_batch * dec_seq, heads_b * hd)
            wo = swa_wo[b].reshape(heads_b * hd, d).astype(BF16)
            streams = [_proj_residual(o, wo, vec(swa_bo[b]), h) for o, h in zip((o_p, o_s), streams)]

        last = layer == depth - 1
        streams = [_ffn(h, vec(ffn_norm[layer]), ffn_wg, ffn_wu, ffn_wd, vec(final_norm), layer, final_norm=last)
                   for h in streams]

    y_prompt = streams[0].reshape(batch, seq, d)
    y_sample = streams[1].reshape(dec_batch, dec_seq, d)
    n_tail = min(WINDOW, seq)
    kv_win_prompt = kv_p.reshape(batch, seq, -1)[:, seq - n_tail:].reshape(batch, n_tail, 2, n_kv, hd)
    kv_win_sample = jnp.transpose(kv_win_sample.reshape(dec_batch, 2, n_kv, hd, n_win), (0, 4, 1, 2, 3))
    return (y_prompt, y_sample, jnp.stack(rows_p_all, axis=0), jnp.stack(rows_s_all, axis=0),
            kv_win_prompt, kv_win_sample)
```
